```python
import math
import jax, jax.numpy as jnp
from jax import lax
import numpy as np

D_MODEL = 1024
BATCH = 8
SEQ = 2048
DEPTH = 1

GDN_HEADS = 8
GDN_HEAD_DIM = 128
GDN_WIDTH = GDN_HEADS * GDN_HEAD_DIM
CONV_K = 5
GLA_HEADS = 4
GLA_KEY_DIM = D_MODEL // 2
GLA_VAL_DIM = D_MODEL
GLA_HEAD_K = GLA_KEY_DIM // GLA_HEADS
GLA_HEAD_V = GLA_VAL_DIM // GLA_HEADS
GLA_GATE_RANK = 16
GLA_GATE_NORMALIZER = 16.0
CHUNK = 64
NORM_EPS = 1e-6

IN_SIZES = [
    3 * GDN_WIDTH,
    GDN_WIDTH,
    GDN_HEADS,
    GDN_HEADS,
    GDN_HEADS,
    GDN_HEADS,
    GLA_KEY_DIM,
    GLA_KEY_DIM,
    GLA_VAL_DIM,
    GLA_VAL_DIM,
    GLA_GATE_RANK,
    GLA_GATE_RANK,
    D_MODEL,
    D_MODEL,
]
N_IN = int(sum(IN_SIZES))
IN_SPLITS = [int(s) for s in np.cumsum(IN_SIZES)[:-1]]

kernel_name = "bidir_gdn_gla_gated_hybrid"


def rmsnorm(x, w):
    xf = x.astype(jnp.float32)
    y = xf * lax.rsqrt(jnp.mean(xf * xf, axis=-1, keepdims=True) + NORM_EPS)
    return (y * w.astype(jnp.float32)).astype(x.dtype)


def l2norm(x):
    return x * lax.rsqrt(jnp.sum(x * x, axis=-1, keepdims=True) + NORM_EPS)


def to_heads(x, n_heads):
    b, t, _ = x.shape
    return x.reshape(b, t, n_heads, -1).transpose(0, 2, 1, 3)


def from_heads(x):
    return x.transpose(0, 2, 1, 3)


def to_chunks(x):
    b, h, t = x.shape[:3]
    return x.reshape(b, h, t // CHUNK, CHUNK, *x.shape[3:])


def centred_depthwise_conv(x, w):
    c = x.shape[-1]
    return lax.conv_general_dilated(
        x, w[:, None, :].astype(x.dtype), window_strides=(1,),
        padding=[(CONV_K // 2, CONV_K // 2)],
        dimension_numbers=("NWC", "WIO", "NWC"), feature_group_count=c)


def gated_delta_rule(q, k, v, g, beta):
    bsz, nh, t, dk = q.shape
    dv = v.shape[-1]
    q = to_chunks(q * (dk ** -0.5))
    k = to_chunks(k)
    v = to_chunks(v)
    beta = to_chunks(beta)
    g = jnp.cumsum(to_chunks(g), axis=-1)
    incl = jnp.tril(jnp.ones((CHUNK, CHUNK), dtype=bool))
    strict = jnp.tril(jnp.ones((CHUNK, CHUNK), dtype=bool), -1)
    diff = g[..., :, None] - g[..., None, :]
    decay = jnp.where(incl, jnp.exp(jnp.where(incl, diff, 0.0)), 0.0)
    kb = k * beta[..., None]
    lower = jnp.where(strict, jnp.einsum("bhnid,bhnjd->bhnij", kb, k) * decay, 0.0)
    rhs = jnp.concatenate([v * beta[..., None], kb * jnp.exp(g)[..., None]], axis=-1)
    sol = lax.linalg.triangular_solve(lower, rhs, left_side=True, lower=True,
                                      unit_diagonal=True)
    u, w = sol[..., :dv], sol[..., dv:]
    attn = jnp.einsum("bhnid,bhnjd->bhnij", q, k) * decay
    q_dec = q * jnp.exp(g)[..., None]
    g_last = g[..., -1]
    k_dec = k * jnp.exp(g_last[..., None] - g)[..., None]

    def step(S, inp):
        u_n, w_n, attn_n, qd_n, kd_n, gl_n = inp
        v_new = u_n - jnp.einsum("bhcd,bhde->bhce", w_n, S)
        o = (jnp.einsum("bhcd,bhde->bhce", qd_n, S)
             + jnp.einsum("bhij,bhje->bhie", attn_n, v_new))
        S = S * jnp.exp(gl_n)[..., None, None] + jnp.einsum("bhcd,bhce->bhde", kd_n, v_new)
        return S, o

    xs = tuple(jnp.moveaxis(a, 2, 0) for a in (u, w, attn, q_dec, k_dec, g_last))
    S0 = jnp.zeros((bsz, nh, dk, dv), q.dtype)
    _, o = lax.scan(step, S0, xs)
    return jnp.moveaxis(o, 0, 2).reshape(bsz, nh, t, dv)


def gla_chunked(q, k, v, gk):
    bsz, nh, t, dk = q.shape
    dv = v.shape[-1]
    q = to_chunks(q * (dk ** -0.5))
    k = to_chunks(k)
    v = to_chunks(v)
    G = jnp.cumsum(to_chunks(gk), axis=3)
    qg = q * jnp.exp(G)
    kg = k * jnp.exp(-G)
    incl = jnp.tril(jnp.ones((CHUNK, CHUNK), dtype=bool))
    attn = jnp.where(incl, jnp.einsum("bhnid,bhnjd->bhnij", qg, kg), 0.0)
    intra = jnp.einsum("bhnij,bhnje->bhnie", attn, v)
    G_last = G[..., -1, :]
    k_dec = k * jnp.exp(G_last[..., None, :] - G)

    def step(S, inp):
        qg_n, kd_n, v_n, gl_n = inp
        o = jnp.einsum("bhcd,bhde->bhce", qg_n, S)
        S = S * jnp.exp(gl_n)[..., :, None] + jnp.einsum("bhcd,bhce->bhde", kd_n, v_n)
        return S, o

    xs = tuple(jnp.moveaxis(a, 2, 0) for a in (qg, k_dec, v, G_last))
    S0 = jnp.zeros((bsz, nh, dk, dv), q.dtype)
    _, inter = lax.scan(step, S0, xs)
    o = jnp.moveaxis(inter, 0, 2) + intra
    return o.reshape(bsz, nh, t, dv)


def flip_t(a):
    return jnp.flip(a, axis=2)


def hybrid_layer(x, ln_pre_w, w_in, conv_w, a_log_fwd, a_log_bwd, dt_bias_fwd, dt_bias_bwd,
                 gdn_norm_w, w_proj_gdn, gk_w2_fwd, gk_b2_fwd, gk_w2_bwd, gk_b2_bwd,
                 gla_norm_w, w_proj_gla, w_out, ln_post_w):
    f32 = jnp.float32
    h = rmsnorm(x, ln_pre_w)
    proj = h @ w_in
    (qkv_a, z_a, a_f, a_b, b_f, b_b, q_b, k_b, v_b, g_b,
     r_f, r_b, gate_a, gate_b) = jnp.split(proj, IN_SPLITS, axis=-1)

    qkv_a = jax.nn.silu(centred_depthwise_conv(qkv_a, conv_w)).astype(f32)
    q_a, k_a, v_a = jnp.split(qkv_a, 3, axis=-1)
    q_a = l2norm(to_heads(q_a, GDN_HEADS))
    k_a = l2norm(to_heads(k_a, GDN_HEADS))
    v_a = to_heads(v_a, GDN_HEADS)
    lg_f = (-jnp.exp(a_log_fwd.astype(f32)) * jax.nn.softplus(a_f.astype(f32) + dt_bias_fwd.astype(f32))).transpose(0, 2, 1)
    lg_b = (-jnp.exp(a_log_bwd.astype(f32)) * jax.nn.softplus(a_b.astype(f32) + dt_bias_bwd.astype(f32))).transpose(0, 2, 1)
    beta_f = jax.nn.sigmoid(b_f.astype(f32)).transpose(0, 2, 1)
    beta_b = jax.nn.sigmoid(b_b.astype(f32)).transpose(0, 2, 1)
    o_a = (gated_delta_rule(q_a, k_a, v_a, lg_f, beta_f)
           + flip_t(gated_delta_rule(flip_t(q_a), flip_t(k_a), flip_t(v_a),
                                     flip_t(lg_b), flip_t(beta_b))))
    o_a = rmsnorm(from_heads(o_a), gdn_norm_w)
    o_a = o_a * jax.nn.silu(z_a.astype(f32)).reshape(o_a.shape)
    y_a = o_a.reshape(x.shape[0], x.shape[1], GDN_WIDTH).astype(x.dtype) @ w_proj_gdn

    q_bh = to_heads(q_b.astype(f32), GLA_HEADS)
    k_bh = to_heads(k_b.astype(f32), GLA_HEADS)
    v_bh = to_heads(v_b.astype(f32), GLA_HEADS)
    gk_f = jax.nn.log_sigmoid((r_f @ gk_w2_fwd + gk_b2_fwd).astype(f32)) / GLA_GATE_NORMALIZER
    gk_b = jax.nn.log_sigmoid((r_b @ gk_w2_bwd + gk_b2_bwd).astype(f32)) / GLA_GATE_NORMALIZER
    gk_f = to_heads(gk_f, GLA_HEADS)
    gk_b = to_heads(gk_b, GLA_HEADS)
    o_b = (gla_chunked(q_bh, k_bh, v_bh, gk_f)
           + flip_t(gla_chunked(flip_t(q_bh), flip_t(k_bh), flip_t(v_bh), flip_t(gk_b))))
    o_b = rmsnorm(from_heads(o_b), gla_norm_w)
    o_b = o_b * jax.nn.silu(g_b.astype(f32)).reshape(o_b.shape)
    y_b = o_b.reshape(x.shape[0], x.shape[1], GLA_VAL_DIM).astype(x.dtype) @ w_proj_gla

    merged = jax.nn.sigmoid(gate_a) * y_a + jax.nn.sigmoid(gate_b) * y_b
    out = merged @ w_out
    return x + rmsnorm(out, ln_post_w)


def setup_inputs(seed: int = 0) -> dict:
    key = jax.random.key(seed)
    ks = jax.random.split(key, 20)
    L, D = DEPTH, D_MODEL

    def nrm(k, shape, scale):
        return jax.random.normal(k, shape, jnp.float32) * scale

    def dt_bias(k):
        u = jax.random.uniform(k, (L, GDN_HEADS), jnp.float32)
        dt = jnp.exp(u * (math.log(0.1) - math.log(0.001)) + math.log(0.001))
        return dt + jnp.log(-jnp.expm1(-dt))

    def a_log(k):
        return jnp.log(jax.random.uniform(k, (L, GDN_HEADS), jnp.float32, 1.0, 16.0))

    return {
        "x": nrm(ks[0], (BATCH, SEQ, D), 1.0),
        "ln_pre_w": 1.0 + nrm(ks[1], (L, D), 0.02),
        "w_in": nrm(ks[2], (L, D, N_IN), D ** -0.5),
        "conv_w": nrm(ks[3], (L, CONV_K, 3 * GDN_WIDTH), CONV_K ** -0.5),
        "a_log_fwd": a_log(ks[4]),
        "a_log_bwd": a_log(ks[5]),
        "dt_bias_fwd": dt_bias(ks[6]),
        "dt_bias_bwd": dt_bias(ks[7]),
        "gdn_norm_w": 1.0 + nrm(ks[8], (L, GDN_HEAD_DIM), 0.02),
        "w_proj_gdn": nrm(ks[9], (L, GDN_WIDTH, D), GDN_WIDTH ** -0.5),
        "gk_w2_fwd": nrm(ks[10], (L, GLA_GATE_RANK, GLA_KEY_DIM), GLA_GATE_RANK ** -0.5),
        "gk_b2_fwd": nrm(ks[11], (L, GLA_KEY_DIM), 0.01),
        "gk_w2_bwd": nrm(ks[12], (L, GLA_GATE_RANK, GLA_KEY_DIM), GLA_GATE_RANK ** -0.5),
        "gk_b2_bwd": nrm(ks[13], (L, GLA_KEY_DIM), 0.01),
        "gla_norm_w": 1.0 + nrm(ks[14], (L, GLA_HEAD_V), 0.02),
        "w_proj_gla": nrm(ks[15], (L, GLA_VAL_DIM, D), GLA_VAL_DIM ** -0.5),
        "w_out": nrm(ks[16], (L, D, D), D ** -0.5),
        "ln_post_w": 1.0 + nrm(ks[17], (L, D), 0.02),
    }


def reference(x, ln_pre_w, w_in, conv_w, a_log_fwd, a_log_bwd, dt_bias_fwd, dt_bias_bwd,
              gdn_norm_w, w_proj_gdn, gk_w2_fwd, gk_b2_fwd, gk_w2_bwd, gk_b2_bwd,
              gla_norm_w, w_proj_gla, w_out, ln_post_w):
    h = x
    for l in range(DEPTH):
        h = hybrid_layer(h, ln_pre_w[l], w_in[l], conv_w[l], a_log_fwd[l], a_log_bwd[l],
                         dt_bias_fwd[l], dt_bias_bwd[l], gdn_norm_w[l], w_proj_gdn[l],
                         gk_w2_fwd[l], gk_b2_fwd[l], gk_w2_bwd[l], gk_b2_bwd[l],
                         gla_norm_w[l], w_proj_gla[l], w_out[l], ln_post_w[l])
    return h
```

```python
import functools

import jax
import jax.numpy as jnp
from jax import lax
from jax.experimental import pallas as pl
from jax.experimental.pallas import tpu as pltpu

F32 = jnp.float32
BF16 = jnp.bfloat16

D_MODEL = 1024
GDN_HEADS = 8
GDN_HEAD_DIM = 128
GDN_WIDTH = GDN_HEADS * GDN_HEAD_DIM
CONV_K = 5
GLA_HEADS = 4
GLA_KEY_DIM = D_MODEL // 2
GLA_VAL_DIM = D_MODEL
GLA_HEAD_K = GLA_KEY_DIM // GLA_HEADS
GLA_HEAD_V = GLA_VAL_DIM // GLA_HEADS
GLA_GATE_RANK = 16
GLA_GATE_NORMALIZER = 16.0
GLA_CHUNK = 64
NORM_EPS = 1e-6

LANES = 128
GDN_CHUNK = LANES
VMEM_LIMIT = 56 * 1024 * 1024

_IN_SIZES = [3 * GDN_WIDTH, GDN_WIDTH, GDN_HEADS, GDN_HEADS, GDN_HEADS, GDN_HEADS,
             GLA_KEY_DIM, GLA_KEY_DIM, GLA_VAL_DIM, GLA_VAL_DIM,
             GLA_GATE_RANK, GLA_GATE_RANK, D_MODEL, D_MODEL]
_OFF = [0]
for _s in _IN_SIZES:
    _OFF.append(_OFF[-1] + _s)
N_BIG = 3 * GDN_WIDTH + GDN_WIDTH + 2 * GLA_KEY_DIM + 2 * GLA_VAL_DIM + 2 * D_MODEL
COL_Z = 3 * GDN_WIDTH
COL_QB = COL_Z + GDN_WIDTH
COL_KB = COL_QB + GLA_KEY_DIM
COL_VB = COL_KB + GLA_KEY_DIM
COL_GB = COL_VB + GLA_VAL_DIM
COL_GATE_A = COL_GB + GLA_VAL_DIM
COL_GATE_B = COL_GATE_A + D_MODEL
LANE_GF, LANE_GB, LANE_BF, LANE_BB = 0, 8, 16, 24
LANE_RF, LANE_RB = 32, 48


def _dot(a, b):
    return jnp.dot(a, b, preferred_element_type=F32)


def _dot_nt(a, b):
    return lax.dot_general(a, b, (((1,), (1,)), ((), ())), preferred_element_type=F32)


def _split2(x):
    hi = x.astype(BF16)
    lo = (x - hi.astype(F32)).astype(BF16)
    return hi, lo


def _split3(x):
    hi = x.astype(BF16)
    r = x - hi.astype(F32)
    mid = r.astype(BF16)
    lo = (r - mid.astype(F32)).astype(BF16)
    return hi, mid, lo


def _sigmoid(x):
    return 1.0 / (1.0 + jnp.exp(-x))


def _silu(x):
    return x * _sigmoid(x)


def _softplus(x):
    return jnp.maximum(x, 0.0) + jnp.log(1.0 + jnp.exp(-jnp.abs(x)))


def _log_sigmoid(x):
    return jnp.minimum(x, 0.0) - jnp.log(1.0 + jnp.exp(-jnp.abs(x)))


def _inproj_kernel(x_ref, lnw_ref, wbig_ref, wsmall_ref, pbig_ref, psmall_ref, h_ref, *, tm):
    j = pl.program_id(1)

    @pl.when(j == 0)
    def _():
        def body(r, _):
            rows = pl.ds(pl.multiple_of(r * LANES, LANES), LANES)
            x = x_ref[rows, :]
            ms = jnp.mean(x * x, axis=-1, keepdims=True)
            h_ref[rows, :] = (x * lax.rsqrt(ms + NORM_EPS) * lnw_ref[...]).astype(BF16)
            return 0
        lax.fori_loop(0, tm // LANES, body, 0)
        psmall_ref[...] = _dot(h_ref[...], wsmall_ref[...])

    pbig_ref[...] = _dot(h_ref[...], wbig_ref[...]).astype(BF16)


def _inproj(x2d, ln_w, w_big, w_small, *, tm=1024, tn=1024):
    m, d = x2d.shape
    return pl.pallas_call(
        functools.partial(_inproj_kernel, tm=tm),
        grid=(m // tm, N_BIG // tn),
        in_specs=[
            pl.BlockSpec((tm, d), lambda i, j: (i, 0)),
            pl.BlockSpec((1, d), lambda i, j: (0, 0)),
            pl.BlockSpec((d, tn), lambda i, j: (0, j)),
            pl.BlockSpec((d, LANES), lambda i, j: (0, 0)),
        ],
        out_specs=[
            pl.BlockSpec((tm, tn), lambda i, j: (i, j)),
            pl.BlockSpec((tm, LANES), lambda i, j: (i, 0)),
        ],
        out_shape=[
            jax.ShapeDtypeStruct((m, N_BIG), BF16),
            jax.ShapeDtypeStruct((m, LANES), F32),
        ],
        scratch_shapes=[pltpu.VMEM((tm, d), BF16)],
        compiler_params=pltpu.CompilerParams(
            dimension_semantics=("arbitrary", "arbitrary"),
            vmem_limit_bytes=VMEM_LIMIT),
        name="inproj",
    )(x2d, ln_w, w_big, w_small)


def _gates_kernel(ps_ref, alog_ref, dtb_ref, out_ref, *, seq):
    c = GDN_CHUNK
    ri = lax.broadcasted_iota(jnp.int32, (c, c), 0)
    ci = lax.broadcasted_iota(jnp.int32, (c, c), 1)
    tri_l = jnp.where(ri >= ci, 1.0, 0.0).astype(BF16)
    tri_u = jnp.where(ri <= ci, 1.0, 0.0).astype(BF16)
    lane = lax.broadcasted_iota(jnp.int32, (c, LANES), 1)
    neg_a = -jnp.exp(alog_ref[...])

    def body(n, _):
        rows = pl.ds(pl.multiple_of(n * c, c), c)
        a = ps_ref[rows, :]
        g = neg_a * _softplus(a + dtb_ref[...])
        beta = _sigmoid(a)
        parts = jnp.concatenate(_split3(g), axis=1)
        cl = _dot(tri_l, parts)
        cu = _dot(tri_u, parts)
        cl = cl[:, :LANES] + cl[:, LANES:2 * LANES] + cl[:, 2 * LANES:]
        cu = cu[:, :LANES] + cu[:, LANES:2 * LANES] + cu[:, 2 * LANES:]
        out_ref[rows, :] = jnp.where(
            lane < LANE_GB, cl, jnp.where(lane < LANE_BF, cu, jnp.where(lane < LANE_RF, beta, a)))
        return 0

    lax.fori_loop(0, seq // c, body, 0)


def _gates(ps, alog_vec, dtb_vec):
    b, t, _ = ps.shape
    return pl.pallas_call(
        functools.partial(_gates_kernel, seq=t),
        grid=(b,),
        in_specs=[
            pl.BlockSpec((None, t, LANES), lambda i: (i, 0, 0)),
            pl.BlockSpec((1, LANES), lambda i: (0, 0)),
            pl.BlockSpec((1, LANES), lambda i: (0, 0)),
        ],
        out_specs=pl.BlockSpec((None, t, LANES), lambda i: (i, 0, 0)),
        out_shape=jax.ShapeDtypeStruct((b, t, LANES), F32),
        compiler_params=pltpu.CompilerParams(
            dimension_semantics=("arbitrary",), vmem_limit_bytes=VMEM_LIMIT),
        name="gates",
    )(ps, alog_vec, dtb_vec)


def _mm_split(a, b):
    c = a.shape[1]
    ah, al = _split2(a)
    bh, bl = _split2(b)
    r = _dot(ah, jnp.concatenate([bh, bl], axis=1))
    return r[:, :c] + r[:, c:] + _dot(al, bh)


def _unit_tri_inverse(l_mat, masks_ref, eye):
    t = eye - l_mat * masks_ref[0]
    for lvl in range(1, 7):
        cm = l_mat * masks_ref[lvl]
        t = t - _mm_split(t, _mm_split(cm, t))
    return t


def _gdn_kernel(q_ref, k_ref, v_ref, z_ref, gc_ref, cwq_ref, cwk_ref, cwv_ref, nw_ref, o_ref,
                xpad, qs, ks, vs, us, wqs, aks, egls, osum, masks, selc, selr, *, seq):
    c = GDN_CHUNK
    nc = seq // c
    h = pl.program_id(1)
    ri = lax.broadcasted_iota(jnp.int32, (c, c), 0)
    ci = lax.broadcasted_iota(jnp.int32, (c, c), 1)
    eye = jnp.where(ri == ci, 1.0, 0.0)

    for lvl in range(7):
        m = ((ri >> (lvl + 1)) == (ci >> (lvl + 1))) & ((ri >> lvl) != (ci >> lvl))
        masks[lvl] = jnp.where(m, 1.0, 0.0)

    l2 = lax.broadcasted_iota(jnp.int32, (2 * c, 2 * c), 0) & (c - 1)
    cb = lax.broadcasted_iota(jnp.int32, (2 * c, 2 * c), 1) >> 7
    lr = lax.broadcasted_iota(jnp.int32, (c, 2 * c), 1) & (c - 1)
    for d in range(2):
        lane_g = h + (LANE_GF if d == 0 else LANE_GB)
        lane_b = h + (LANE_BF if d == 0 else LANE_BB)
        selc[d] = jnp.where(l2 == jnp.where(cb == 0, lane_g, lane_b), 1.0, 0.0).astype(BF16)
        selr[d] = jnp.where(lr == lane_g, 1.0, 0.0).astype(BF16)

    zeros8 = jnp.zeros((8, LANES), F32)

    def conv_silu(x_ref, cw_ref):
        xpad[pl.ds(0, 8), :] = zeros8
        xpad[pl.ds(seq + 8, 8), :] = zeros8
        xpad[pl.ds(8, seq), :] = x_ref[...].astype(F32)
        acc = cw_ref[0:1, :] * xpad[pl.ds(8 - CONV_K // 2, seq), :]
        for j in range(1, CONV_K):
            acc = acc + cw_ref[j:j + 1, :] * xpad[pl.ds(8 - CONV_K // 2 + j, seq), :]
        return _silu(acc)

    def l2norm(y):
        return y * lax.rsqrt(jnp.sum(y * y, axis=-1, keepdims=True) + NORM_EPS)

    qs[...] = l2norm(conv_silu(q_ref, cwq_ref)) * (GDN_HEAD_DIM ** -0.5)
    ks[...] = l2norm(conv_silu(k_ref, cwk_ref))
    vs[...] = conv_silu(v_ref, cwv_ref)
    osum[...] = jnp.zeros((seq, LANES), F32)

    def prep(n, _):
        rows = pl.ds(pl.multiple_of(n * c, c), c)
        qc = qs[rows, :]
        kc = ks[rows, :]
        vc = vs[rows, :]
        kcb = kc.astype(BF16)
        kk = _dot_nt(kcb, kcb)
        qk = _dot_nt(qc.astype(BF16), kcb)
        gch = jnp.concatenate(_split2(gc_ref[rows, :]), axis=1)
        for d in range(2):
            sel = _dot(gch, selc[d])
            gcol = sel[:, :c]
            beta = sel[:, c:]
            grow = _dot_nt(selr[d], gch)
            incl = (ri >= ci) if d == 0 else (ri <= ci)
            strict = (ri > ci) if d == 0 else (ri < ci)
            decay = jnp.where(incl, jnp.exp(jnp.where(incl, gcol - grow, 0.0)), 0.0)
            l_mat = jnp.where(strict, kk * decay, 0.0) * beta
            t_inv = _unit_tri_inverse(l_mat, masks, eye)
            eg = jnp.exp(gcol)
            kb = kc * beta
            rhs = jnp.concatenate([(vc * beta).astype(BF16), (kb * eg).astype(BF16)], axis=1)
            sol = _dot(t_inv.astype(BF16), rhs)
            glast = gcol[c - 1:c, :] if d == 0 else gcol[0:1, :]
            kd = kc * jnp.exp(glast - gcol)
            us[d, rows, :] = sol[:, :c]
            wqs[d, n, 0:c, :] = sol[:, c:].astype(BF16)
            wqs[d, n, c:2 * c, :] = (qc * eg).astype(BF16)
            aks[d, n, 0:c, :] = (qk * decay).astype(BF16)
            aks[d, n, c:2 * c, :] = kd.T.astype(BF16)
            egls[d, pl.ds(pl.multiple_of(n * 8, 8), 8), :] = jnp.broadcast_to(jnp.exp(glast), (8, LANES))
        return 0

    lax.fori_loop(0, nc, prep, 0)

    def scan(n, carry):
        new = []
        for d in range(2):
            s = carry[d]
            cc = n if d == 0 else nc - 1 - n
            rows = pl.ds(pl.multiple_of(cc * c, c), c)
            r1 = _dot(wqs[d, cc], s.astype(BF16))
            v_new = us[d, rows, :] - r1[:c]
            r2 = _dot(aks[d, cc], v_new.astype(BF16))
            osum[rows, :] += r1[c:] + r2[:c]
            new.append(s * egls[d, pl.ds(pl.multiple_of(cc * 8, 8), 1), :] + r2[c:])
        return tuple(new)

    s0 = jnp.zeros((GDN_HEAD_DIM, GDN_HEAD_DIM), F32)
    lax.fori_loop(0, nc, scan, (s0, s0))

    o = osum[...]
    y = o * lax.rsqrt(jnp.mean(o * o, axis=-1, keepdims=True) + NORM_EPS) * nw_ref[...]
    o_ref[...] = (y * _silu(z_ref[...].astype(F32))).astype(BF16)


def _gdn(pbig, gc, conv_w, norm_w):
    b, t, _ = pbig.shape
    c = GDN_CHUNK
    nc = t // c
    hd = GDN_HEAD_DIM
    nh = GDN_HEADS
    return pl.pallas_call(
        functools.partial(_gdn_kernel, seq=t),
        grid=(b, nh),
        in_specs=[
            pl.BlockSpec((None, t, hd), lambda i, j: (i, 0, j)),
            pl.BlockSpec((None, t, hd), lambda i, j: (i, 0, nh + j)),
            pl.BlockSpec((None, t, hd), lambda i, j: (i, 0, 2 * nh + j)),
            pl.BlockSpec((None, t, hd), lambda i, j: (i, 0, COL_Z // hd + j)),
            pl.BlockSpec((None, t, LANES), lambda i, j: (i, 0, 0)),
            pl.BlockSpec((CONV_K, hd), lambda i, j: (0, j)),
            pl.BlockSpec((CONV_K, hd), lambda i, j: (0, nh + j)),
            pl.BlockSpec((CONV_K, hd), lambda i, j: (0, 2 * nh + j)),
            pl.BlockSpec((1, hd), lambda i, j: (0, 0)),
        ],
        out_specs=pl.BlockSpec((None, t, hd), lambda i, j: (i, 0, j)),
        out_shape=jax.ShapeDtypeStruct((b, t, GDN_WIDTH), BF16),
        scratch_shapes=[
            pltpu.VMEM((t + 16, hd), F32),
            pltpu.VMEM((t, hd), F32),
            pltpu.VMEM((t, hd), F32),
            pltpu.VMEM((t, hd), F32),
            pltpu.VMEM((2, t, hd), F32),
            pltpu.VMEM((2, nc, 2 * c, hd), BF16),
            pltpu.VMEM((2, nc, 2 * c, hd), BF16),
            pltpu.VMEM((2, nc * 8, LANES), F32),
            pltpu.VMEM((t, hd), F32),
            pltpu.VMEM((7, c, c), F32),
            pltpu.VMEM((2, 2 * c, 2 * c), BF16),
            pltpu.VMEM((2, c, 2 * c), BF16),
        ],
        compiler_params=pltpu.CompilerParams(
            dimension_semantics=("arbitrary", "arbitrary"), vmem_limit_bytes=VMEM_LIMIT),
        name="gdn",
    )(pbig, pbig, pbig, pbig, gc, conv_w, conv_w, conv_w, norm_w)


def _gla_kernel(q_ref, k_ref, v_ref, g_ref, gc_ref, w2f_ref, w2b_ref, b2f_ref, b2b_ref, nw_ref,
                o_ref, gks, sst, osum, *, seq):
    c2 = 2 * GLA_CHUNK
    npair = seq // c2
    dk = GLA_HEAD_K
    dv = GLA_HEAD_V
    ri = lax.broadcasted_iota(jnp.int32, (c2, c2), 0)
    ci = lax.broadcasted_iota(jnp.int32, (c2, c2), 1)
    same = (ri >> 6) == (ci >> 6)
    first_row = lax.broadcasted_iota(jnp.int32, (c2, dk), 0) < GLA_CHUNK
    first_row_v = lax.broadcasted_iota(jnp.int32, (c2, dv), 0) < GLA_CHUNK
    first_lane = lax.broadcasted_iota(jnp.int32, (dk, c2), 1) < GLA_CHUNK

    def gate(n, _):
        rows = pl.ds(pl.multiple_of(n * c2, c2), c2)
        rb = gc_ref[rows, :].astype(BF16)
        gks[0, rows, :] = _log_sigmoid(_dot(rb, w2f_ref[...]) + b2f_ref[...]) / GLA_GATE_NORMALIZER
        gks[1, rows, :] = _log_sigmoid(_dot(rb, w2b_ref[...]) + b2b_ref[...]) / GLA_GATE_NORMALIZER
        return 0

    lax.fori_loop(0, npair, gate, 0)
    osum[...] = jnp.zeros((seq, dv), F32)
    sst[...] = jnp.zeros((2, dk, dv), F32)

    def pair(n, _):
        for d in range(2):
            pp = n if d == 0 else npair - 1 - n
            rows = pl.ds(pl.multiple_of(pp * c2, c2), c2)
            incl = same & ((ri >= ci) if d == 0 else (ri <= ci))
            tri = jnp.where(incl, 1.0, 0.0).astype(BF16)
            parts = jnp.concatenate(_split3(gks[d, rows, :]), axis=1)
            cs = _dot(tri, parts)
            gcum = cs[:, :dk] + cs[:, dk:2 * dk] + cs[:, 2 * dk:]
            q = q_ref[rows, :].astype(F32) * (dk ** -0.5)
            k = k_ref[rows, :].astype(F32)
            vb = v_ref[rows, :]
            qg = (q * jnp.exp(gcum)).astype(BF16)
            kg = (k * jnp.exp(-gcum)).astype(BF16)
            attn = jnp.where(incl, _dot_nt(qg, kg), 0.0)
            intra = _dot(attn.astype(BF16), vb)
            if d == 0:
                last_a, last_b = gcum[GLA_CHUNK - 1:GLA_CHUNK, :], gcum[c2 - 1:c2, :]
            else:
                last_a, last_b = gcum[0:1, :], gcum[GLA_CHUNK:GLA_CHUNK + 1, :]
            glast = jnp.where(first_row, last_a, last_b)
            kdt = (k * jnp.exp(glast - gcum)).T
            xa = _dot(jnp.where(first_lane, kdt, 0.0).astype(BF16), vb)
            xb = _dot(jnp.where(first_lane, 0.0, kdt).astype(BF16), vb)
            gt = gcum.T
            if d == 0:
                ea = jnp.exp(gt[:, GLA_CHUNK - 1:GLA_CHUNK])
                eb = jnp.exp(gt[:, c2 - 1:c2])
            else:
                ea = jnp.exp(gt[:, 0:1])
                eb = jnp.exp(gt[:, GLA_CHUNK:GLA_CHUNK + 1])
            s0 = sst[d]
            if d == 0:
                o_a = _dot(qg, s0.astype(BF16))
                s1 = s0 * ea + xa
                o_b = _dot(qg, s1.astype(BF16))
                s2 = s1 * eb + xb
            else:
                o_b = _dot(qg, s0.astype(BF16))
                s1 = s0 * eb + xb
                o_a = _dot(qg, s1.astype(BF16))
                s2 = s1 * ea + xa
            sst[d] = s2
            osum[rows, :] += jnp.where(first_row_v, o_a, o_b) + intra
        return 0

    lax.fori_loop(0, npair, pair, 0)

    o = osum[...]
    y = o * lax.rsqrt(jnp.mean(o * o, axis=-1, keepdims=True) + NORM_EPS) * nw_ref[...]
    o_ref[...] = (y * _silu(g_ref[...].astype(F32))).astype(BF16)


def _gla(pbig, gc, w2f, w2b, b2f, b2b, norm_w):
    b, t, _ = pbig.shape
    dk = GLA_HEAD_K
    dv = GLA_HEAD_V
    return pl.pallas_call(
        functools.partial(_gla_kernel, seq=t),
        grid=(b, GLA_HEADS),
        in_specs=[
            pl.BlockSpec((None, t, dk), lambda i, j: (i, 0, COL_QB // dk + j)),
            pl.BlockSpec((None, t, dk), lambda i, j: (i, 0, COL_KB // dk + j)),
            pl.BlockSpec((None, t, dv), lambda i, j: (i, 0, COL_VB // dv + j)),
            pl.BlockSpec((None, t, dv), lambda i, j: (i, 0, COL_GB // dv + j)),
            pl.BlockSpec((None, t, LANES), lambda i, j: (i, 0, 0)),
            pl.BlockSpec((LANES, dk), lambda i, j: (0, j)),
            pl.BlockSpec((LANES, dk), lambda i, j: (0, j)),
            pl.BlockSpec((1, dk), lambda i, j: (0, j)),
            pl.BlockSpec((1, dk), lambda i, j: (0, j)),
            pl.BlockSpec((1, dv), lambda i, j: (0, 0)),
        ],
        out_specs=pl.BlockSpec((None, t, dv), lambda i, j: (i, 0, j)),
        out_shape=jax.ShapeDtypeStruct((b, t, GLA_VAL_DIM), BF16),
        scratch_shapes=[
            pltpu.VMEM((2, t, dk), F32),
            pltpu.VMEM((2, dk, dv), F32),
            pltpu.VMEM((t, dv), F32),
        ],
        compiler_params=pltpu.CompilerParams(
            dimension_semantics=("arbitrary", "arbitrary"), vmem_limit_bytes=VMEM_LIMIT),
        name="gla",
    )(pbig, pbig, pbig, pbig, gc, w2f, w2b, b2f, b2b, norm_w)


def _merge_kernel(oa_ref, ob_ref, ga_ref, gb_ref, x_ref, wa_ref, wb_ref, wo_ref, lnw_ref, out_ref):
    ya = _dot(oa_ref[...], wa_ref[...])
    yb = _dot(ob_ref[...], wb_ref[...])
    merged = (_sigmoid(ga_ref[...].astype(F32)) * ya + _sigmoid(gb_ref[...].astype(F32)) * yb)
    out = _dot(merged.astype(BF16), wo_ref[...])
    y = out * lax.rsqrt(jnp.mean(out * out, axis=-1, keepdims=True) + NORM_EPS) * lnw_ref[...]
    out_ref[...] = x_ref[...] + y


def _merge(oa, ob, pbig2d, x2d, wa, wb, wo, ln_w, *, tm=512):
    m, d = x2d.shape
    return pl.pallas_call(
        _merge_kernel,
        grid=(m // tm,),
        in_specs=[
            pl.BlockSpec((tm, d), lambda i: (i, 0)),
            pl.BlockSpec((tm, d), lambda i: (i, 0)),
            pl.BlockSpec((tm, d), lambda i: (i, COL_GATE_A // D_MODEL)),
            pl.BlockSpec((tm, d), lambda i: (i, COL_GATE_B // D_MODEL)),
            pl.BlockSpec((tm, d), lambda i: (i, 0)),
            pl.BlockSpec((d, d), lambda i: (0, 0)),
            pl.BlockSpec((d, d), lambda i: (0, 0)),
            pl.BlockSpec((d, d), lambda i: (0, 0)),
            pl.BlockSpec((1, d), lambda i: (0, 0)),
        ],
        out_specs=pl.BlockSpec((tm, d), lambda i: (i, 0)),
        out_shape=jax.ShapeDtypeStruct((m, d), F32),
        compiler_params=pltpu.CompilerParams(
            dimension_semantics=("arbitrary",), vmem_limit_bytes=VMEM_LIMIT),
        name="merge",
    )(oa, ob, pbig2d, pbig2d, x2d, wa, wb, wo, ln_w)


def _layer(x, ln_pre_w, w_in, conv_w, a_log_fwd, a_log_bwd, dt_bias_fwd, dt_bias_bwd,
           gdn_norm_w, w_proj_gdn, gk_w2_fwd, gk_b2_fwd, gk_w2_bwd, gk_b2_bwd,
           gla_norm_w, w_proj_gla, w_out, ln_post_w):
    b, t, d = x.shape
    m = b * t
    o = _OFF
    w_big = jnp.concatenate([w_in[:, o[0]:o[2]], w_in[:, o[6]:o[10]], w_in[:, o[12]:o[14]]],
                            axis=1).astype(BF16)
    w_small = jnp.concatenate(
        [w_in[:, o[2]:o[6]], w_in[:, o[10]:o[12]],
         jnp.zeros((d, LANES - 4 * GDN_HEADS - 2 * GLA_GATE_RANK), w_in.dtype)], axis=1).astype(BF16)
    pad = jnp.zeros((LANES - 2 * GDN_HEADS,), F32)
    alog_vec = jnp.concatenate([a_log_fwd, a_log_bwd, pad]).reshape(1, LANES)
    dtb_vec = jnp.concatenate([dt_bias_fwd, dt_bias_bwd, pad]).reshape(1, LANES)

    def w2_ext(w2, lane0):
        z = jnp.zeros((LANES, GLA_KEY_DIM), F32)
        return z.at[lane0:lane0 + GLA_GATE_RANK, :].set(w2).astype(BF16)

    x2d = x.reshape(m, d)
    pbig, psmall = _inproj(x2d, ln_pre_w.reshape(1, d), w_big, w_small)
    gc = _gates(psmall.reshape(b, t, LANES), alog_vec, dtb_vec)
    pbig3 = pbig.reshape(b, t, N_BIG)
    oa = _gdn(pbig3, gc, conv_w, gdn_norm_w.reshape(1, GDN_HEAD_DIM))
    ob = _gla(pbig3, gc, w2_ext(gk_w2_fwd, LANE_RF), w2_ext(gk_w2_bwd, LANE_RB),
              gk_b2_fwd.reshape(1, GLA_KEY_DIM), gk_b2_bwd.reshape(1, GLA_KEY_DIM),
              gla_norm_w.reshape(1, GLA_HEAD_V))
    out = _merge(oa.reshape(m, GDN_WIDTH), ob.reshape(m, GLA_VAL_DIM), pbig, x2d,
                 w_proj_gdn.astype(BF16), w_proj_gla.astype(BF16), w_out.astype(BF16),
                 ln_post_w.reshape(1, d))
    return out.reshape(b, t, d)


def kernel(x, ln_pre_w, w_in, conv_w, a_log_fwd, a_log_bwd, dt_bias_fwd, dt_bias_bwd, gdn_norm_w,
           w_proj_gdn, gk_w2_fwd, gk_b2_fwd, gk_w2_bwd, gk_b2_bwd, gla_norm_w, w_proj_gla, w_out,
           ln_post_w):
    h = x
    for l in range(ln_pre_w.shape[0]):
        h = _layer(h, ln_pre_w[l], w_in[l], conv_w[l], a_log_fwd[l], a_log_bwd[l],
                   dt_bias_fwd[l], dt_bias_bwd[l], gdn_norm_w[l], w_proj_gdn[l],
                   gk_w2_fwd[l], gk_b2_fwd[l], gk_w2_bwd[l], gk_b2_bwd[l],
                   gla_norm_w[l], w_proj_gla[l], w_out[l], ln_post_w[l])
    return h
```

```python
import functools

import jax
import jax.numpy as jnp
from jax import lax
from jax.experimental import pallas as pl
from jax.experimental.pallas import tpu as pltpu

F32 = jnp.float32
BF16 = jnp.bfloat16

D_MODEL = 1024
GDN_HEADS = 8
GDN_HEAD_DIM = 128
GDN_WIDTH = GDN_HEADS * GDN_HEAD_DIM
CONV_K = 5
GLA_HEADS = 4
GLA_KEY_DIM = D_MODEL // 2
GLA_VAL_DIM = D_MODEL
GLA_HEAD_K = GLA_KEY_DIM // GLA_HEADS
GLA_HEAD_V = GLA_VAL_DIM // GLA_HEADS
GLA_GATE_RANK = 16
GLA_GATE_NORMALIZER = 16.0
GLA_CHUNK = 64
NORM_EPS = 1e-6

LANES = 128
GDN_CHUNK = LANES
VMEM_LIMIT = 56 * 1024 * 1024

_IN_SIZES = [3 * GDN_WIDTH, GDN_WIDTH, GDN_HEADS, GDN_HEADS, GDN_HEADS, GDN_HEADS,
             GLA_KEY_DIM, GLA_KEY_DIM, GLA_VAL_DIM, GLA_VAL_DIM,
             GLA_GATE_RANK, GLA_GATE_RANK, D_MODEL, D_MODEL]
_OFF = [0]
for _s in _IN_SIZES:
    _OFF.append(_OFF[-1] + _s)
N_BIG = 3 * GDN_WIDTH + GDN_WIDTH + 2 * GLA_KEY_DIM + 2 * GLA_VAL_DIM + 2 * D_MODEL
COL_Z = 3 * GDN_WIDTH
COL_QB = COL_Z + GDN_WIDTH
COL_KB = COL_QB + GLA_KEY_DIM
COL_VB = COL_KB + GLA_KEY_DIM
COL_GB = COL_VB + GLA_VAL_DIM
COL_GATE_A = COL_GB + GLA_VAL_DIM
COL_GATE_B = COL_GATE_A + D_MODEL
LANE_GF, LANE_GB, LANE_BF, LANE_BB = 0, 8, 16, 24
LANE_RF, LANE_RB = 32, 48


def _dot(a, b):
    return jnp.dot(a, b, preferred_element_type=F32)


def _dot_nt(a, b):
    return lax.dot_general(a, b, (((1,), (1,)), ((), ())), preferred_element_type=F32)


def _split2(x):
    hi = x.astype(BF16)
    lo = (x - hi.astype(F32)).astype(BF16)
    return hi, lo


def _split3(x):
    hi = x.astype(BF16)
    r = x - hi.astype(F32)
    mid = r.astype(BF16)
    lo = (r - mid.astype(F32)).astype(BF16)
    return hi, mid, lo


def _sigmoid(x):
    return 1.0 / (1.0 + jnp.exp(-x))


def _silu(x):
    return x * _sigmoid(x)


def _softplus(x):
    return jnp.maximum(x, 0.0) + jnp.log(1.0 + jnp.exp(-jnp.abs(x)))


def _log_sigmoid(x):
    return jnp.minimum(x, 0.0) - jnp.log(1.0 + jnp.exp(-jnp.abs(x)))


def _inproj_kernel(x_ref, lnw_ref, wbig_ref, wsmall_ref, pbig_ref, psmall_ref, h_ref, *, tm):
    j = pl.program_id(1)

    @pl.when(j == 0)
    def _():
        def body(r, _):
            rows = pl.ds(pl.multiple_of(r * LANES, LANES), LANES)
            x = x_ref[rows, :]
            ms = jnp.mean(x * x, axis=-1, keepdims=True)
            h_ref[rows, :] = (x * lax.rsqrt(ms + NORM_EPS) * lnw_ref[...]).astype(BF16)
            return 0
        lax.fori_loop(0, tm // LANES, body, 0)
        psmall_ref[...] = _dot(h_ref[...], wsmall_ref[...])

    pbig_ref[...] = _dot(h_ref[...], wbig_ref[...]).astype(BF16)


def _inproj(x2d, ln_w, w_big, w_small, *, tm=1024, tn=1024):
    m, d = x2d.shape
    return pl.pallas_call(
        functools.partial(_inproj_kernel, tm=tm),
        grid=(m // tm, N_BIG // tn),
        in_specs=[
            pl.BlockSpec((tm, d), lambda i, j: (i, 0)),
            pl.BlockSpec((1, d), lambda i, j: (0, 0)),
            pl.BlockSpec((d, tn), lambda i, j: (0, j)),
            pl.BlockSpec((d, LANES), lambda i, j: (0, 0)),
        ],
        out_specs=[
            pl.BlockSpec((tm, tn), lambda i, j: (i, j)),
            pl.BlockSpec((tm, LANES), lambda i, j: (i, 0)),
        ],
        out_shape=[
            jax.ShapeDtypeStruct((m, N_BIG), BF16),
            jax.ShapeDtypeStruct((m, LANES), F32),
        ],
        scratch_shapes=[pltpu.VMEM((tm, d), BF16)],
        compiler_params=pltpu.CompilerParams(
            dimension_semantics=("arbitrary", "arbitrary"),
            vmem_limit_bytes=VMEM_LIMIT),
        name="inproj",
    )(x2d, ln_w, w_big, w_small)


def _gates_kernel(ps_ref, alog_ref, dtb_ref, out_ref, *, seq):
    c = GDN_CHUNK
    ri = lax.broadcasted_iota(jnp.int32, (c, c), 0)
    ci = lax.broadcasted_iota(jnp.int32, (c, c), 1)
    tri_l = jnp.where(ri >= ci, 1.0, 0.0).astype(BF16)
    tri_u = jnp.where(ri <= ci, 1.0, 0.0).astype(BF16)
    lane = lax.broadcasted_iota(jnp.int32, (c, LANES), 1)
    neg_a = -jnp.exp(alog_ref[...])

    def body(n, _):
        rows = pl.ds(pl.multiple_of(n * c, c), c)
        a = ps_ref[rows, :]
        g = neg_a * _softplus(a + dtb_ref[...])
        beta = _sigmoid(a)
        parts = jnp.concatenate(_split3(g), axis=1)
        cl = _dot(tri_l, parts)
        cu = _dot(tri_u, parts)
        cl = cl[:, :LANES] + cl[:, LANES:2 * LANES] + cl[:, 2 * LANES:]
        cu = cu[:, :LANES] + cu[:, LANES:2 * LANES] + cu[:, 2 * LANES:]
        out_ref[rows, :] = jnp.where(
            lane < LANE_GB, cl, jnp.where(lane < LANE_BF, cu, jnp.where(lane < LANE_RF, beta, a)))
        return 0

    lax.fori_loop(0, seq // c, body, 0)


def _gates(ps, alog_vec, dtb_vec):
    b, t, _ = ps.shape
    return pl.pallas_call(
        functools.partial(_gates_kernel, seq=t),
        grid=(b,),
        in_specs=[
            pl.BlockSpec((None, t, LANES), lambda i: (i, 0, 0)),
            pl.BlockSpec((1, LANES), lambda i: (0, 0)),
            pl.BlockSpec((1, LANES), lambda i: (0, 0)),
        ],
        out_specs=pl.BlockSpec((None, t, LANES), lambda i: (i, 0, 0)),
        out_shape=jax.ShapeDtypeStruct((b, t, LANES), F32),
        compiler_params=pltpu.CompilerParams(
            dimension_semantics=("arbitrary",), vmem_limit_bytes=VMEM_LIMIT),
        name="gates",
    )(ps, alog_vec, dtb_vec)


def _block_diag(xp):
    c = xp.shape[0]
    z = jnp.zeros((c, c), xp.dtype)
    return jnp.concatenate([jnp.concatenate([xp[:, :c], z], axis=1),
                            jnp.concatenate([z, xp[:, c:]], axis=1)], axis=0)


def _unit_tri_inverse_packed(lps, masks_ref, eye_p):
    ts = [eye_p - lp * masks_ref[0] for lp in lps]
    for lvl in range(1, 7):
        ms = [_dot((lp * masks_ref[lvl]).astype(BF16), _block_diag(t.astype(BF16)))
              for lp, t in zip(lps, ts)]
        ts = [t - _dot(t.astype(BF16), _block_diag(m.astype(BF16))) for t, m in zip(ts, ms)]
    return ts


def _gdn_kernel(q_ref, k_ref, v_ref, z_ref, gc_ref, cwq_ref, cwk_ref, cwv_ref, nw_ref, o_ref,
                xpad, qs, ks, vs, us, wqs, aks, egls, osum, masks, selc, selr, *, seq, unroll):
    c = GDN_CHUNK
    nc = seq // c
    h = pl.program_id(1)
    ri = lax.broadcasted_iota(jnp.int32, (c, c), 0)
    ci = lax.broadcasted_iota(jnp.int32, (c, c), 1)
    rp = lax.broadcasted_iota(jnp.int32, (c, 2 * c), 0)
    cp = lax.broadcasted_iota(jnp.int32, (c, 2 * c), 1) & (c - 1)
    eye_p = jnp.where(rp == cp, 1.0, 0.0)

    for lvl in range(7):
        m = ((rp >> (lvl + 1)) == (cp >> (lvl + 1))) & ((rp >> lvl) != (cp >> lvl))
        masks[lvl] = jnp.where(m, 1.0, 0.0)

    l2 = lax.broadcasted_iota(jnp.int32, (2 * c, 2 * c), 0) & (c - 1)
    cb = lax.broadcasted_iota(jnp.int32, (2 * c, 2 * c), 1) >> 7
    lr = lax.broadcasted_iota(jnp.int32, (c, 2 * c), 1) & (c - 1)
    for d in range(2):
        lane_g = h + (LANE_GF if d == 0 else LANE_GB)
        lane_b = h + (LANE_BF if d == 0 else LANE_BB)
        selc[d] = jnp.where(l2 == jnp.where(cb == 0, lane_g, lane_b), 1.0, 0.0).astype(BF16)
        selr[d] = jnp.where(lr == lane_g, 1.0, 0.0).astype(BF16)

    zeros8 = jnp.zeros((8, LANES), F32)

    def conv_silu(x_ref, cw_ref):
        xpad[pl.ds(0, 8), :] = zeros8
        xpad[pl.ds(seq + 8, 8), :] = zeros8
        xpad[pl.ds(8, seq), :] = x_ref[...].astype(F32)
        acc = cw_ref[0:1, :] * xpad[pl.ds(8 - CONV_K // 2, seq), :]
        for j in range(1, CONV_K):
            acc = acc + cw_ref[j:j + 1, :] * xpad[pl.ds(8 - CONV_K // 2 + j, seq), :]
        return _silu(acc)

    def l2norm(y):
        return y * lax.rsqrt(jnp.sum(y * y, axis=-1, keepdims=True) + NORM_EPS)

    qs[...] = l2norm(conv_silu(q_ref, cwq_ref)) * (GDN_HEAD_DIM ** -0.5)
    ks[...] = l2norm(conv_silu(k_ref, cwk_ref))
    vs[...] = conv_silu(v_ref, cwv_ref)
    osum[...] = jnp.zeros((seq, LANES), F32)

    def prep(it, _):
        chunks = []
        for u in range(unroll):
            n = it * unroll + u
            rows = pl.ds(pl.multiple_of(n * c, c), c)
            qc = qs[rows, :]
            kc = ks[rows, :]
            kcb = kc.astype(BF16)
            kk = _dot_nt(kcb, kcb)
            qk = _dot_nt(qc.astype(BF16), kcb)
            gch = jnp.concatenate(_split2(gc_ref[rows, :]), axis=1)
            per_dir = []
            for d in range(2):
                sel = _dot(gch, selc[d])
                gcol = sel[:, :c]
                beta = sel[:, c:]
                grow = _dot_nt(selr[d], gch)
                incl = (ri >= ci) if d == 0 else (ri <= ci)
                strict = (ri > ci) if d == 0 else (ri < ci)
                decay = jnp.where(incl, jnp.exp(jnp.where(incl, gcol - grow, 0.0)), 0.0)
                l_mat = jnp.where(strict, kk * decay, 0.0) * beta
                aks[d, n, 0:c, :] = (qk * decay).astype(BF16)
                per_dir.append((gcol, beta, l_mat))
            chunks.append((n, rows, per_dir))

        lps = [jnp.concatenate([pd[0][2], pd[1][2]], axis=1) for _, _, pd in chunks]
        tps = _unit_tri_inverse_packed(lps, masks, eye_p)

        for (n, rows, per_dir), tp in zip(chunks, tps):
            qc = qs[rows, :]
            kc = ks[rows, :]
            vc = vs[rows, :]
            for d in range(2):
                gcol, beta, _ = per_dir[d]
                t_inv = tp[:, d * c:(d + 1) * c]
                eg = jnp.exp(gcol)
                kb = kc * beta
                rhs = jnp.concatenate([(vc * beta).astype(BF16), (kb * eg).astype(BF16)], axis=1)
                sol = _dot(t_inv.astype(BF16), rhs)
                glast = gcol[c - 1:c, :] if d == 0 else gcol[0:1, :]
                kd = kc * jnp.exp(glast - gcol)
                us[d, rows, :] = sol[:, :c]
                wqs[d, n, 0:c, :] = sol[:, c:].astype(BF16)
                wqs[d, n, c:2 * c, :] = (qc * eg).astype(BF16)
                aks[d, n, c:2 * c, :] = kd.T.astype(BF16)
                egls[d, pl.ds(pl.multiple_of(n * 8, 8), 8), :] = jnp.broadcast_to(
                    jnp.exp(glast), (8, LANES))
        return 0

    lax.fori_loop(0, nc // unroll, prep, 0)

    def scan(n, carry):
        new = []
        for d in range(2):
            s = carry[d]
            cc = n if d == 0 else nc - 1 - n
            rows = pl.ds(pl.multiple_of(cc * c, c), c)
            r1 = _dot(wqs[d, cc], s.astype(BF16))
            v_new = us[d, rows, :] - r1[:c]
            r2 = _dot(aks[d, cc], v_new.astype(BF16))
            osum[rows, :] += r1[c:] + r2[:c]
            new.append(s * egls[d, pl.ds(pl.multiple_of(cc * 8, 8), 1), :] + r2[c:])
        return tuple(new)

    s0 = jnp.zeros((GDN_HEAD_DIM, GDN_HEAD_DIM), F32)
    lax.fori_loop(0, nc, scan, (s0, s0))

    o = osum[...]
    y = o * lax.rsqrt(jnp.mean(o * o, axis=-1, keepdims=True) + NORM_EPS) * nw_ref[...]
    o_ref[...] = (y * _silu(z_ref[...].astype(F32))).astype(BF16)


def _gdn(pbig, gc, conv_w, norm_w, *, unroll=8):
    b, t, _ = pbig.shape
    c = GDN_CHUNK
    nc = t // c
    hd = GDN_HEAD_DIM
    nh = GDN_HEADS
    return pl.pallas_call(
        functools.partial(_gdn_kernel, seq=t, unroll=unroll),
        grid=(b, nh),
        in_specs=[
            pl.BlockSpec((None, t, hd), lambda i, j: (i, 0, j)),
            pl.BlockSpec((None, t, hd), lambda i, j: (i, 0, nh + j)),
            pl.BlockSpec((None, t, hd), lambda i, j: (i, 0, 2 * nh + j)),
            pl.BlockSpec((None, t, hd), lambda i, j: (i, 0, COL_Z // hd + j)),
            pl.BlockSpec((None, t, LANES), lambda i, j: (i, 0, 0)),
            pl.BlockSpec((CONV_K, hd), lambda i, j: (0, j)),
            pl.BlockSpec((CONV_K, hd), lambda i, j: (0, nh + j)),
            pl.BlockSpec((CONV_K, hd), lambda i, j: (0, 2 * nh + j)),
            pl.BlockSpec((1, hd), lambda i, j: (0, 0)),
        ],
        out_specs=pl.BlockSpec((None, t, hd), lambda i, j: (i, 0, j)),
        out_shape=jax.ShapeDtypeStruct((b, t, GDN_WIDTH), BF16),
        scratch_shapes=[
            pltpu.VMEM((t + 16, hd), F32),
            pltpu.VMEM((t, hd), F32),
            pltpu.VMEM((t, hd), F32),
            pltpu.VMEM((t, hd), F32),
            pltpu.VMEM((2, t, hd), F32),
            pltpu.VMEM((2, nc, 2 * c, hd), BF16),
            pltpu.VMEM((2, nc, 2 * c, hd), BF16),
            pltpu.VMEM((2, nc * 8, LANES), F32),
            pltpu.VMEM((t, hd), F32),
            pltpu.VMEM((7, c, 2 * c), F32),
            pltpu.VMEM((2, 2 * c, 2 * c), BF16),
            pltpu.VMEM((2, c, 2 * c), BF16),
        ],
        compiler_params=pltpu.CompilerParams(
            dimension_semantics=("arbitrary", "arbitrary"), vmem_limit_bytes=VMEM_LIMIT),
        name="gdn",
    )(pbig, pbig, pbig, pbig, gc, conv_w, conv_w, conv_w, norm_w)


def _gla_kernel(q_ref, k_ref, v_ref, g_ref, gc_ref, w2f_ref, w2b_ref, b2f_ref, b2b_ref, nw_ref,
                o_ref, tris, sst, osum, *, seq, unroll):
    c2 = 2 * GLA_CHUNK
    npair = seq // c2
    dk = GLA_HEAD_K
    dv = GLA_HEAD_V
    ri = lax.broadcasted_iota(jnp.int32, (c2, c2), 0)
    ci = lax.broadcasted_iota(jnp.int32, (c2, c2), 1)
    same = (ri >> 6) == (ci >> 6)
    incls = [same & (ri >= ci), same & (ri <= ci)]
    first_row = lax.broadcasted_iota(jnp.int32, (c2, dk), 0) < GLA_CHUNK
    first_row_v = lax.broadcasted_iota(jnp.int32, (c2, dv), 0) < GLA_CHUNK
    first_lane = lax.broadcasted_iota(jnp.int32, (dk, c2), 1) < GLA_CHUNK
    w2_refs = (w2f_ref, w2b_ref)
    b2_refs = (b2f_ref, b2b_ref)

    for d in range(2):
        tris[d] = jnp.where(incls[d], 1.0, 0.0).astype(BF16)
    osum[...] = jnp.zeros((seq, dv), F32)
    sst[...] = jnp.zeros((2, dk, dv), F32)

    def pairs(it, _):
        items = []
        for u in range(unroll):
            for d in range(2):
                pp = it * unroll + u
                pp = pp if d == 0 else npair - 1 - pp
                rows = pl.ds(pl.multiple_of(pp * c2, c2), c2)
                pre = _dot(gc_ref[rows, :].astype(BF16), w2_refs[d][...]) + b2_refs[d][...]
                items.append([d, rows, pre])
        for x in items:
            gk = _log_sigmoid(x[2]) / GLA_GATE_NORMALIZER
            cs = _dot(tris[x[0]], jnp.concatenate(_split3(gk), axis=1))
            x[2] = cs[:, :dk] + cs[:, dk:2 * dk] + cs[:, 2 * dk:]
        for x in items:
            d, rows, gcum = x
            q = q_ref[rows, :].astype(F32) * (dk ** -0.5)
            k = k_ref[rows, :].astype(F32)
            qg = (q * jnp.exp(gcum)).astype(BF16)
            kg = (k * jnp.exp(-gcum)).astype(BF16)
            attn = jnp.where(incls[d], _dot_nt(qg, kg), 0.0)
            if d == 0:
                last_a, last_b = gcum[GLA_CHUNK - 1:GLA_CHUNK, :], gcum[c2 - 1:c2, :]
            else:
                last_a, last_b = gcum[0:1, :], gcum[GLA_CHUNK:GLA_CHUNK + 1, :]
            glast = jnp.where(first_row, last_a, last_b)
            kdt = (k * jnp.exp(glast - gcum)).T
            gt = gcum.T
            if d == 0:
                ea = jnp.exp(gt[:, GLA_CHUNK - 1:GLA_CHUNK])
                eb = jnp.exp(gt[:, c2 - 1:c2])
            else:
                ea = jnp.exp(gt[:, 0:1])
                eb = jnp.exp(gt[:, GLA_CHUNK:GLA_CHUNK + 1])
            x.extend([qg, attn, kdt, ea, eb])
        for x in items:
            d, rows, _, qg, attn, kdt, ea, eb = x
            vb = v_ref[rows, :]
            intra = _dot(attn.astype(BF16), vb)
            xa = _dot(jnp.where(first_lane, kdt, 0.0).astype(BF16), vb)
            xb = _dot(jnp.where(first_lane, 0.0, kdt).astype(BF16), vb)
            x[4:6] = [intra, xa]
            x.append(xb)
        states = [sst[0], sst[1]]
        for x in items:
            d, rows, _, qg, intra, xa, ea, eb, xb = x
            s0 = states[d]
            if d == 0:
                o_a = _dot(qg, s0.astype(BF16))
                s1 = s0 * ea + xa
                o_b = _dot(qg, s1.astype(BF16))
                s2 = s1 * eb + xb
            else:
                o_b = _dot(qg, s0.astype(BF16))
                s1 = s0 * eb + xb
                o_a = _dot(qg, s1.astype(BF16))
                s2 = s1 * ea + xa
            states[d] = s2
            osum[rows, :] += jnp.where(first_row_v, o_a, o_b) + intra
        sst[0] = states[0]
        sst[1] = states[1]
        return 0

    lax.fori_loop(0, npair // unroll, pairs, 0)

    o = osum[...]
    y = o * lax.rsqrt(jnp.mean(o * o, axis=-1, keepdims=True) + NORM_EPS) * nw_ref[...]
    o_ref[...] = (y * _silu(g_ref[...].astype(F32))).astype(BF16)


def _gla(pbig, gc, w2f, w2b, b2f, b2b, norm_w, *, unroll=4):
    b, t, _ = pbig.shape
    dk = GLA_HEAD_K
    dv = GLA_HEAD_V
    return pl.pallas_call(
        functools.partial(_gla_kernel, seq=t, unroll=unroll),
        grid=(b, GLA_HEADS),
        in_specs=[
            pl.BlockSpec((None, t, dk), lambda i, j: (i, 0, COL_QB // dk + j)),
            pl.BlockSpec((None, t, dk), lambda i, j: (i, 0, COL_KB // dk + j)),
            pl.BlockSpec((None, t, dv), lambda i, j: (i, 0, COL_VB // dv + j)),
            pl.BlockSpec((None, t, dv), lambda i, j: (i, 0, COL_GB // dv + j)),
            pl.BlockSpec((None, t, LANES), lambda i, j: (i, 0, 0)),
            pl.BlockSpec((LANES, dk), lambda i, j: (0, j)),
            pl.BlockSpec((LANES, dk), lambda i, j: (0, j)),
            pl.BlockSpec((1, dk), lambda i, j: (0, j)),
            pl.BlockSpec((1, dk), lambda i, j: (0, j)),
            pl.BlockSpec((1, dv), lambda i, j: (0, 0)),
        ],
        out_specs=pl.BlockSpec((None, t, dv), lambda i, j: (i, 0, j)),
        out_shape=jax.ShapeDtypeStruct((b, t, GLA_VAL_DIM), BF16),
        scratch_shapes=[
            pltpu.VMEM((2, 2 * GLA_CHUNK, 2 * GLA_CHUNK), BF16),
            pltpu.VMEM((2, dk, dv), F32),
            pltpu.VMEM((t, dv), F32),
        ],
        compiler_params=pltpu.CompilerParams(
            dimension_semantics=("arbitrary", "arbitrary"), vmem_limit_bytes=VMEM_LIMIT),
        name="gla",
    )(pbig, pbig, pbig, pbig, gc, w2f, w2b, b2f, b2b, norm_w)


def _merge_kernel(oa_ref, ob_ref, ga_ref, gb_ref, x_ref, wa_ref, wb_ref, wo_ref, lnw_ref, out_ref):
    ya = _dot(oa_ref[...], wa_ref[...])
    yb = _dot(ob_ref[...], wb_ref[...])
    merged = (_sigmoid(ga_ref[...].astype(F32)) * ya + _sigmoid(gb_ref[...].astype(F32)) * yb)
    out = _dot(merged.astype(BF16), wo_ref[...])
    y = out * lax.rsqrt(jnp.mean(out * out, axis=-1, keepdims=True) + NORM_EPS) * lnw_ref[...]
    out_ref[...] = x_ref[...] + y


def _merge(oa, ob, pbig2d, x2d, wa, wb, wo, ln_w, *, tm=512):
    m, d = x2d.shape
    return pl.pallas_call(
        _merge_kernel,
        grid=(m // tm,),
        in_specs=[
            pl.BlockSpec((tm, d), lambda i: (i, 0)),
            pl.BlockSpec((tm, d), lambda i: (i, 0)),
            pl.BlockSpec((tm, d), lambda i: (i, COL_GATE_A // D_MODEL)),
            pl.BlockSpec((tm, d), lambda i: (i, COL_GATE_B // D_MODEL)),
            pl.BlockSpec((tm, d), lambda i: (i, 0)),
            pl.BlockSpec((d, d), lambda i: (0, 0)),
            pl.BlockSpec((d, d), lambda i: (0, 0)),
            pl.BlockSpec((d, d), lambda i: (0, 0)),
            pl.BlockSpec((1, d), lambda i: (0, 0)),
        ],
        out_specs=pl.BlockSpec((tm, d), lambda i: (i, 0)),
        out_shape=jax.ShapeDtypeStruct((m, d), F32),
        compiler_params=pltpu.CompilerParams(
            dimension_semantics=("arbitrary",), vmem_limit_bytes=VMEM_LIMIT),
        name="merge",
    )(oa, ob, pbig2d, pbig2d, x2d, wa, wb, wo, ln_w)


def _layer(x, ln_pre_w, w_in, conv_w, a_log_fwd, a_log_bwd, dt_bias_fwd, dt_bias_bwd,
           gdn_norm_w, w_proj_gdn, gk_w2_fwd, gk_b2_fwd, gk_w2_bwd, gk_b2_bwd,
           gla_norm_w, w_proj_gla, w_out, ln_post_w):
    b, t, d = x.shape
    m = b * t
    o = _OFF
    w_big = jnp.concatenate([w_in[:, o[0]:o[2]], w_in[:, o[6]:o[10]], w_in[:, o[12]:o[14]]],
                            axis=1).astype(BF16)
    w_small = jnp.concatenate(
        [w_in[:, o[2]:o[6]], w_in[:, o[10]:o[12]],
         jnp.zeros((d, LANES - 4 * GDN_HEADS - 2 * GLA_GATE_RANK), w_in.dtype)], axis=1).astype(BF16)
    pad = jnp.zeros((LANES - 2 * GDN_HEADS,), F32)
    alog_vec = jnp.concatenate([a_log_fwd, a_log_bwd, pad]).reshape(1, LANES)
    dtb_vec = jnp.concatenate([dt_bias_fwd, dt_bias_bwd, pad]).reshape(1, LANES)

    def w2_ext(w2, lane0):
        z = jnp.zeros((LANES, GLA_KEY_DIM), F32)
        return z.at[lane0:lane0 + GLA_GATE_RANK, :].set(w2).astype(BF16)

    x2d = x.reshape(m, d)
    pbig, psmall = _inproj(x2d, ln_pre_w.reshape(1, d), w_big, w_small)
    gc = _gates(psmall.reshape(b, t, LANES), alog_vec, dtb_vec)
    pbig3 = pbig.reshape(b, t, N_BIG)
    oa = _gdn(pbig3, gc, conv_w, gdn_norm_w.reshape(1, GDN_HEAD_DIM))
    ob = _gla(pbig3, gc, w2_ext(gk_w2_fwd, LANE_RF), w2_ext(gk_w2_bwd, LANE_RB),
              gk_b2_fwd.reshape(1, GLA_KEY_DIM), gk_b2_bwd.reshape(1, GLA_KEY_DIM),
              gla_norm_w.reshape(1, GLA_HEAD_V))
    out = _merge(oa.reshape(m, GDN_WIDTH), ob.reshape(m, GLA_VAL_DIM), pbig, x2d,
                 w_proj_gdn.astype(BF16), w_proj_gla.astype(BF16), w_out.astype(BF16),
                 ln_post_w.reshape(1, d))
    return out.reshape(b, t, d)


def kernel(x, ln_pre_w, w_in, conv_w, a_log_fwd, a_log_bwd, dt_bias_fwd, dt_bias_bwd, gdn_norm_w,
           w_proj_gdn, gk_w2_fwd, gk_b2_fwd, gk_w2_bwd, gk_b2_bwd, gla_norm_w, w_proj_gla, w_out,
           ln_post_w):
    h = x
    for l in range(ln_pre_w.shape[0]):
        h = _layer(h, ln_pre_w[l], w_in[l], conv_w[l], a_log_fwd[l], a_log_bwd[l],
                   dt_bias_fwd[l], dt_bias_bwd[l], gdn_norm_w[l], w_proj_gdn[l],
                   gk_w2_fwd[l], gk_b2_fwd[l], gk_w2_bwd[l], gk_b2_bwd[l],
                   gla_norm_w[l], w_proj_gla[l], w_out[l], ln_post_w[l])
    return h
```

```python
import functools

import jax
import jax.numpy as jnp
from jax import lax
from jax.experimental import pallas as pl
from jax.experimental.pallas import tpu as pltpu

F32 = jnp.float32
BF16 = jnp.bfloat16

D_MODEL = 1024
GDN_HEADS = 8
GDN_HEAD_DIM = 128
GDN_WIDTH = GDN_HEADS * GDN_HEAD_DIM
CONV_K = 5
GLA_HEADS = 4
GLA_KEY_DIM = D_MODEL // 2
GLA_VAL_DIM = D_MODEL
GLA_HEAD_K = GLA_KEY_DIM // GLA_HEADS
GLA_HEAD_V = GLA_VAL_DIM // GLA_HEADS
GLA_GATE_RANK = 16
GLA_GATE_NORMALIZER = 16.0
GLA_CHUNK = 64
NORM_EPS = 1e-6

LANES = 128
GDN_CHUNK = LANES
VMEM_LIMIT = 56 * 1024 * 1024

_IN_SIZES = [3 * GDN_WIDTH, GDN_WIDTH, GDN_HEADS, GDN_HEADS, GDN_HEADS, GDN_HEADS,
             GLA_KEY_DIM, GLA_KEY_DIM, GLA_VAL_DIM, GLA_VAL_DIM,
             GLA_GATE_RANK, GLA_GATE_RANK, D_MODEL, D_MODEL]
_OFF = [0]
for _s in _IN_SIZES:
    _OFF.append(_OFF[-1] + _s)
N_BIG = 3 * GDN_WIDTH + GDN_WIDTH + 2 * GLA_KEY_DIM + 2 * GLA_VAL_DIM + 2 * D_MODEL
COL_Z = 3 * GDN_WIDTH
COL_QB = COL_Z + GDN_WIDTH
COL_KB = COL_QB + GLA_KEY_DIM
COL_VB = COL_KB + GLA_KEY_DIM
COL_GB = COL_VB + GLA_VAL_DIM
COL_GATE_A = COL_GB + GLA_VAL_DIM
COL_GATE_B = COL_GATE_A + D_MODEL
LANE_GF, LANE_GB, LANE_BF, LANE_BB = 0, 8, 16, 24
LANE_RF, LANE_RB = 32, 48


def _dot(a, b):
    return jnp.dot(a, b, preferred_element_type=F32)


def _dot_nt(a, b):
    return lax.dot_general(a, b, (((1,), (1,)), ((), ())), preferred_element_type=F32)


def _split2(x):
    hi = x.astype(BF16)
    lo = (x - hi.astype(F32)).astype(BF16)
    return hi, lo


def _split3(x):
    hi = x.astype(BF16)
    r = x - hi.astype(F32)
    mid = r.astype(BF16)
    lo = (r - mid.astype(F32)).astype(BF16)
    return hi, mid, lo


def _sigmoid(x):
    return 1.0 / (1.0 + jnp.exp(-x))


def _silu(x):
    return x * _sigmoid(x)


def _softplus(x):
    return jnp.maximum(x, 0.0) + jnp.log(1.0 + jnp.exp(-jnp.abs(x)))


def _log_sigmoid(x):
    return jnp.minimum(x, 0.0) - jnp.log(1.0 + jnp.exp(-jnp.abs(x)))


def _inproj_kernel(x_ref, lnw_ref, wbig_ref, wsmall_ref, pbig_ref, psmall_ref, h_ref, *, tm):
    j = pl.program_id(1)

    @pl.when(j == 0)
    def _():
        def body(r, _):
            rows = pl.ds(pl.multiple_of(r * LANES, LANES), LANES)
            x = x_ref[rows, :]
            ms = jnp.mean(x * x, axis=-1, keepdims=True)
            h_ref[rows, :] = (x * lax.rsqrt(ms + NORM_EPS) * lnw_ref[...]).astype(BF16)
            return 0
        lax.fori_loop(0, tm // LANES, body, 0)
        psmall_ref[...] = _dot(h_ref[...], wsmall_ref[...])

    pbig_ref[...] = _dot(h_ref[...], wbig_ref[...]).astype(BF16)


def _inproj(x2d, ln_w, w_big, w_small, *, tm=1024, tn=2304):
    m, d = x2d.shape
    return pl.pallas_call(
        functools.partial(_inproj_kernel, tm=tm),
        grid=(m // tm, N_BIG // tn),
        in_specs=[
            pl.BlockSpec((tm, d), lambda i, j: (i, 0)),
            pl.BlockSpec((1, d), lambda i, j: (0, 0)),
            pl.BlockSpec((d, tn), lambda i, j: (0, j)),
            pl.BlockSpec((d, LANES), lambda i, j: (0, 0)),
        ],
        out_specs=[
            pl.BlockSpec((tm, tn), lambda i, j: (i, j)),
            pl.BlockSpec((tm, LANES), lambda i, j: (i, 0)),
        ],
        out_shape=[
            jax.ShapeDtypeStruct((m, N_BIG), BF16),
            jax.ShapeDtypeStruct((m, LANES), F32),
        ],
        scratch_shapes=[pltpu.VMEM((tm, d), BF16)],
        compiler_params=pltpu.CompilerParams(
            dimension_semantics=("arbitrary", "arbitrary"),
            vmem_limit_bytes=VMEM_LIMIT),
        name="inproj",
    )(x2d, ln_w, w_big, w_small)


def _gates_kernel(ps_ref, alog_ref, dtb_ref, out_ref, out_t_ref, *, seq):
    c = GDN_CHUNK
    ri = lax.broadcasted_iota(jnp.int32, (c, c), 0)
    ci = lax.broadcasted_iota(jnp.int32, (c, c), 1)
    tri_l = jnp.where(ri >= ci, 1.0, 0.0).astype(BF16)
    tri_u = jnp.where(ri <= ci, 1.0, 0.0).astype(BF16)
    lane = lax.broadcasted_iota(jnp.int32, (c, LANES), 1)
    neg_a = -jnp.exp(alog_ref[...])

    for n in range(seq // c):
        a = ps_ref[n * c:(n + 1) * c, :]
        g = neg_a * _softplus(a + dtb_ref[...])
        beta = _sigmoid(a)
        parts = jnp.concatenate(_split3(g), axis=1)
        cl = _dot(tri_l, parts)
        cu = _dot(tri_u, parts)
        cl = cl[:, :LANES] + cl[:, LANES:2 * LANES] + cl[:, 2 * LANES:]
        cu = cu[:, :LANES] + cu[:, LANES:2 * LANES] + cu[:, 2 * LANES:]
        res = jnp.where(
            lane < LANE_GB, cl, jnp.where(lane < LANE_BF, cu, jnp.where(lane < LANE_RF, beta, a)))
        out_ref[n * c:(n + 1) * c, :] = res
        out_t_ref[:, n * c:(n + 1) * c] = res.T


def _gates(ps, alog_vec, dtb_vec):
    b, t, _ = ps.shape
    return pl.pallas_call(
        functools.partial(_gates_kernel, seq=t),
        grid=(b,),
        in_specs=[
            pl.BlockSpec((None, t, LANES), lambda i: (i, 0, 0)),
            pl.BlockSpec((1, LANES), lambda i: (0, 0)),
            pl.BlockSpec((1, LANES), lambda i: (0, 0)),
        ],
        out_specs=[
            pl.BlockSpec((None, t, LANES), lambda i: (i, 0, 0)),
            pl.BlockSpec((None, LANES, t), lambda i: (i, 0, 0)),
        ],
        out_shape=[
            jax.ShapeDtypeStruct((b, t, LANES), F32),
            jax.ShapeDtypeStruct((b, LANES, t), F32),
        ],
        compiler_params=pltpu.CompilerParams(
            dimension_semantics=("arbitrary",), vmem_limit_bytes=VMEM_LIMIT),
        name="gates",
    )(ps, alog_vec, dtb_vec)


def _block_diag(xp):
    c = xp.shape[0]
    z = jnp.zeros((c, c), xp.dtype)
    return jnp.concatenate([jnp.concatenate([xp[:, :c], z], axis=1),
                            jnp.concatenate([z, xp[:, c:]], axis=1)], axis=0)


def _odd_blocks(x, s):
    return jnp.concatenate([x[b * s:(b + 1) * s] for b in range(1, x.shape[0] // s, 2)], axis=0)


def _spread_odd(x_odd, s):
    z = jnp.zeros((s, x_odd.shape[1]), x_odd.dtype)
    parts = []
    for j in range(x_odd.shape[0] // s):
        parts += [z, x_odd[j * s:(j + 1) * s]]
    return jnp.concatenate(parts, axis=0)


def _sub_from_odd(x, r_odd, s):
    parts = []
    for b in range(x.shape[0] // s):
        blk = x[b * s:(b + 1) * s]
        if b % 2:
            blk = blk - r_odd[(b // 2) * s:(b // 2 + 1) * s]
        parts.append(blk)
    return jnp.concatenate(parts, axis=0)


GDN_FULL_LEVELS = 3


def _unit_tri_inverse_stages(lps, masks_ref, modd_ref, eye_p, out):
    ts = [eye_p - lp * masks_ref[0] for lp in lps]
    for lvl in range(1, 7):
        s = 1 << lvl
        if lvl < GDN_FULL_LEVELS:
            ms = [_dot((lp * masks_ref[lvl]).astype(BF16), _block_diag(t.astype(BF16)))
                  for lp, t in zip(lps, ts)]
            yield
            ts = [t - _dot(t.astype(BF16), _block_diag(m.astype(BF16))) for t, m in zip(ts, ms)]
        else:
            mo = modd_ref[lvl - GDN_FULL_LEVELS]
            ms = [_dot((_odd_blocks(lp, s) * mo).astype(BF16), _block_diag(t.astype(BF16)))
                  for lp, t in zip(lps, ts)]
            yield
            ts = [_sub_from_odd(t, _dot(_odd_blocks(t, s).astype(BF16),
                                        _block_diag(_spread_odd(m, s).astype(BF16))), s)
                  for t, m in zip(ts, ms)]
        yield
    out.extend(ts)


def _gdn_kernel(q_ref, k_ref, v_ref, z_ref, gc_ref, gct_ref, cwq_ref, cwk_ref, cwv_ref, nw_ref,
                o_ref, xpad, qs, ks, vs, kks, qks, aqs, bos, egls, osum, masks, modd, selc, rowbuf,
                *, seq, unroll):
    c = GDN_CHUNK
    nc = seq // c
    h = pl.program_id(1)
    ri = lax.broadcasted_iota(jnp.int32, (c, c), 0)
    ci = lax.broadcasted_iota(jnp.int32, (c, c), 1)
    lower, lower_strict, upper = ri >= ci, ri > ci, ri <= ci
    rp = lax.broadcasted_iota(jnp.int32, (c, 2 * c), 0)
    cp = lax.broadcasted_iota(jnp.int32, (c, 2 * c), 1) & (c - 1)
    eye_p = jnp.where(rp == cp, 1.0, 0.0)

    for lvl in range(7):
        m = jnp.where(((rp >> (lvl + 1)) == (cp >> (lvl + 1))) & ((rp >> lvl) != (cp >> lvl)), 1.0, 0.0)
        if lvl < GDN_FULL_LEVELS:
            masks[lvl] = m
        else:
            modd[lvl - GDN_FULL_LEVELS] = _odd_blocks(m, 1 << lvl)

    l2 = lax.broadcasted_iota(jnp.int32, (2 * c, 2 * c), 0) & (c - 1)
    cb = lax.broadcasted_iota(jnp.int32, (2 * c, 2 * c), 1) >> 7
    lanes_g = (h + LANE_GF, h + LANE_GB)
    lanes_b = (h + LANE_BF, h + LANE_BB)
    for d in range(2):
        selc[d] = jnp.where(l2 == jnp.where(cb == 0, lanes_g[d], lanes_b[d]), 1.0, 0.0).astype(BF16)

    zeros8 = jnp.zeros((8, LANES), F32)

    def conv_silu(x_ref, cw_ref):
        xpad[pl.ds(0, 8), :] = zeros8
        xpad[pl.ds(seq + 8, 8), :] = zeros8
        xpad[pl.ds(8, seq), :] = x_ref[...].astype(F32)
        acc = cw_ref[0:1, :] * xpad[pl.ds(8 - CONV_K // 2, seq), :]
        for j in range(1, CONV_K):
            acc = acc + cw_ref[j:j + 1, :] * xpad[pl.ds(8 - CONV_K // 2 + j, seq), :]
        return _silu(acc)

    def l2norm(y):
        return y * lax.rsqrt(jnp.sum(y * y, axis=-1, keepdims=True) + NORM_EPS)

    qs[...] = l2norm(conv_silu(q_ref, cwq_ref)) * (GDN_HEAD_DIM ** -0.5)
    ks[...] = l2norm(conv_silu(k_ref, cwk_ref))
    for n in range(nc):
        kcb = ks[n * c:(n + 1) * c, :].astype(BF16)
        kks[n] = _dot_nt(kcb, kcb)
        qks[n] = _dot_nt(qs[n * c:(n + 1) * c, :].astype(BF16), kcb)
    vs[...] = conv_silu(v_ref, cwv_ref)
    osum[...] = jnp.zeros((seq, LANES), F32)

    sub = lax.broadcasted_iota(jnp.int32, (8, seq), 0)
    for idx, lane0 in enumerate((LANE_GF, LANE_GB, LANE_BB)):
        rowbuf[idx:idx + 1, :] = jnp.sum(
            jnp.where(sub == h, gct_ref[lane0:lane0 + 8, :], 0.0), axis=0, keepdims=True)

    def row_form(idx, n):
        return jnp.broadcast_to(rowbuf[idx:idx + 1, n * c:(n + 1) * c], (c, c))

    def prep(trip):
        pairs = [(trip * unroll + u, nc - 1 - trip * unroll - u) for u in range(unroll)]
        work = []
        for pair in pairs:
            item = []
            for d, n in enumerate(pair):
                sel = _dot(jnp.concatenate(_split2(gc_ref[n * c:(n + 1) * c, :]), axis=1), selc[d])
                item.append([n, sel[:, :c], sel[:, c:]])
            work.append(item)
        yield
        lps = []
        for (fw, bw) in work:
            n, gcol, beta = fw
            decay = jnp.where(lower, jnp.exp(jnp.where(lower, gcol - row_form(0, n), 0.0)), 0.0)
            l_f = jnp.where(lower_strict, kks[n] * decay, 0.0) * beta
            fw.append(qks[n] * decay)
            n, gcol, beta = bw
            grow = row_form(1, n)
            decay = jnp.where(upper, jnp.exp(jnp.where(upper, gcol - grow, 0.0)), 0.0)
            decay_t = jnp.exp(jnp.where(lower_strict, grow - gcol, 0.0))
            l_b = jnp.where(lower_strict, kks[n] * decay_t, 0.0) * row_form(2, n)
            bw.append(qks[n] * decay)
            lps.append(jnp.concatenate([l_f, l_b], axis=1))
        yield
        tps = []
        yield from _unit_tri_inverse_stages(lps, masks, modd, eye_p, tps)
        sols = []
        for item, tp in zip(work, tps):
            for d, (n, gcol, beta, attn) in enumerate(item):
                t_inv = tp[:, :c] if d == 0 else tp[:, c:].T
                kc = ks[n * c:(n + 1) * c, :]
                eg = jnp.exp(gcol)
                rhs = jnp.concatenate([(vs[n * c:(n + 1) * c, :] * beta).astype(BF16),
                                       (kc * beta * eg).astype(BF16)], axis=1)
                sols.append(_dot(t_inv.astype(BF16), rhs))
        yield
        k = 0
        for item in work:
            for d, (n, gcol, beta, attn) in enumerate(item):
                kc = ks[n * c:(n + 1) * c, :]
                glast = gcol[c - 1:c, :] if d == 0 else gcol[0:1, :]
                kdt = (kc * jnp.exp(glast - gcol)).T
                lhs = jnp.concatenate([kdt.astype(BF16), attn.astype(BF16)], axis=0)
                z = _dot(lhs, sols[k].astype(BF16))
                k += 1
                qd = qs[n * c:(n + 1) * c, :] * jnp.exp(gcol)
                aqs[d, n, 0:c, :] = (-z[:c, c:]).astype(BF16)
                aqs[d, n, c:2 * c, :] = (qd - z[c:, c:]).astype(BF16)
                bos[d, n] = z[:, :c]
                egls[d, n * 8:(n + 1) * 8, :] = jnp.broadcast_to(jnp.exp(glast), (8, LANES))
        yield

    def scan_step(trip, u, states):
        for d in range(2):
            n = trip * unroll + u if d == 0 else nc - 1 - trip * unroll - u
            r = _dot(aqs[d, n], states[d].astype(BF16)) + bos[d, n]
            states[d] = states[d] * egls[d, n * 8:n * 8 + 1, :] + r[:c]
            osum[n * c:(n + 1) * c, :] += r[c:]

    states = [jnp.zeros((GDN_HEAD_DIM, GDN_HEAD_DIM), F32)] * 2
    trips = nc // unroll
    for _ in prep(0):
        pass
    for trip in range(1, trips):
        u = 0
        for _ in prep(trip):
            if u < unroll:
                scan_step(trip - 1, u, states)
                u += 1
        for u in range(u, unroll):
            scan_step(trip - 1, u, states)
    for u in range(unroll):
        scan_step(trips - 1, u, states)

    o = osum[...]
    y = o * lax.rsqrt(jnp.mean(o * o, axis=-1, keepdims=True) + NORM_EPS) * nw_ref[...]
    o_ref[...] = (y * _silu(z_ref[...].astype(F32))).astype(BF16)


def _gdn(pbig, gc, gct, conv_w, norm_w, *, unroll=8):
    b, t, _ = pbig.shape
    c = GDN_CHUNK
    nc = t // c
    hd = GDN_HEAD_DIM
    nh = GDN_HEADS
    assert nc % unroll == 0
    return pl.pallas_call(
        functools.partial(_gdn_kernel, seq=t, unroll=unroll),
        grid=(b, nh),
        in_specs=[
            pl.BlockSpec((None, t, hd), lambda i, j: (i, 0, j)),
            pl.BlockSpec((None, t, hd), lambda i, j: (i, 0, nh + j)),
            pl.BlockSpec((None, t, hd), lambda i, j: (i, 0, 2 * nh + j)),
            pl.BlockSpec((None, t, hd), lambda i, j: (i, 0, COL_Z // hd + j)),
            pl.BlockSpec((None, t, LANES), lambda i, j: (i, 0, 0)),
            pl.BlockSpec((None, LANES, t), lambda i, j: (i, 0, 0)),
            pl.BlockSpec((CONV_K, hd), lambda i, j: (0, j)),
            pl.BlockSpec((CONV_K, hd), lambda i, j: (0, nh + j)),
            pl.BlockSpec((CONV_K, hd), lambda i, j: (0, 2 * nh + j)),
            pl.BlockSpec((1, hd), lambda i, j: (0, 0)),
        ],
        out_specs=pl.BlockSpec((None, t, hd), lambda i, j: (i, 0, j)),
        out_shape=jax.ShapeDtypeStruct((b, t, GDN_WIDTH), BF16),
        scratch_shapes=[
            pltpu.VMEM((t + 16, hd), F32),
            pltpu.VMEM((t, hd), F32),
            pltpu.VMEM((t, hd), F32),
            pltpu.VMEM((t, hd), F32),
            pltpu.VMEM((nc, c, c), F32),
            pltpu.VMEM((nc, c, c), F32),
            pltpu.VMEM((2, nc, 2 * c, hd), BF16),
            pltpu.VMEM((2, nc, 2 * c, hd), F32),
            pltpu.VMEM((2, nc * 8, LANES), F32),
            pltpu.VMEM((t, hd), F32),
            pltpu.VMEM((GDN_FULL_LEVELS, c, 2 * c), F32),
            pltpu.VMEM((7 - GDN_FULL_LEVELS, c // 2, 2 * c), F32),
            pltpu.VMEM((2, 2 * c, 2 * c), BF16),
            pltpu.VMEM((8, t), F32),
        ],
        compiler_params=pltpu.CompilerParams(
            dimension_semantics=("arbitrary", "arbitrary"), vmem_limit_bytes=VMEM_LIMIT),
        name="gdn",
    )(pbig, pbig, pbig, pbig, gc, gct, conv_w, conv_w, conv_w, norm_w)


def _gla_kernel(q_ref, k_ref, v_ref, g_ref, gc_ref, w2f_ref, w2b_ref, b2f_ref, b2b_ref, nw_ref,
                o_ref, tris, sst, osum, *, seq, unroll):
    c2 = 2 * GLA_CHUNK
    npair = seq // c2
    dk = GLA_HEAD_K
    dv = GLA_HEAD_V
    ri = lax.broadcasted_iota(jnp.int32, (c2, c2), 0)
    ci = lax.broadcasted_iota(jnp.int32, (c2, c2), 1)
    same = (ri >> 6) == (ci >> 6)
    incls = [same & (ri >= ci), same & (ri <= ci)]
    first_row = lax.broadcasted_iota(jnp.int32, (c2, dk), 0) < GLA_CHUNK
    first_row_v = lax.broadcasted_iota(jnp.int32, (c2, dv), 0) < GLA_CHUNK
    first_lane = lax.broadcasted_iota(jnp.int32, (dk, c2), 1) < GLA_CHUNK
    w2_refs = (w2f_ref, w2b_ref)
    b2_refs = (b2f_ref, b2b_ref)

    for d in range(2):
        tris[d] = jnp.where(incls[d], 1.0, 0.0).astype(BF16)
    osum[...] = jnp.zeros((seq, dv), F32)
    sst[...] = jnp.zeros((2, dk, dv), F32)

    def pairs(it, _):
        items = []
        for u in range(unroll):
            for d in range(2):
                pp = it * unroll + u
                pp = pp if d == 0 else npair - 1 - pp
                rows = pl.ds(pl.multiple_of(pp * c2, c2), c2)
                pre = _dot(gc_ref[rows, :].astype(BF16), w2_refs[d][...]) + b2_refs[d][...]
                items.append([d, rows, pre])
        for x in items:
            gk = _log_sigmoid(x[2]) / GLA_GATE_NORMALIZER
            cs = _dot(tris[x[0]], jnp.concatenate(_split3(gk), axis=1))
            x[2] = cs[:, :dk] + cs[:, dk:2 * dk] + cs[:, 2 * dk:]
        for x in items:
            d, rows, gcum = x
            q = q_ref[rows, :].astype(F32) * (dk ** -0.5)
            k = k_ref[rows, :].astype(F32)
            qg = (q * jnp.exp(gcum)).astype(BF16)
            kg = (k * jnp.exp(-gcum)).astype(BF16)
            attn = jnp.where(incls[d], _dot_nt(qg, kg), 0.0)
            if d == 0:
                last_a, last_b = gcum[GLA_CHUNK - 1:GLA_CHUNK, :], gcum[c2 - 1:c2, :]
            else:
                last_a, last_b = gcum[0:1, :], gcum[GLA_CHUNK:GLA_CHUNK + 1, :]
            glast = jnp.where(first_row, last_a, last_b)
            kdt = (k * jnp.exp(glast - gcum)).T
            gt = gcum.T
            if d == 0:
                ea = jnp.exp(gt[:, GLA_CHUNK - 1:GLA_CHUNK])
                eb = jnp.exp(gt[:, c2 - 1:c2])
            else:
                ea = jnp.exp(gt[:, 0:1])
                eb = jnp.exp(gt[:, GLA_CHUNK:GLA_CHUNK + 1])
            x.extend([qg, attn, kdt, ea, eb])
        for x in items:
            d, rows, _, qg, attn, kdt, ea, eb = x
            vb = v_ref[rows, :]
            intra = _dot(attn.astype(BF16), vb)
            xa = _dot(jnp.where(first_lane, kdt, 0.0).astype(BF16), vb)
            xb = _dot(jnp.where(first_lane, 0.0, kdt).astype(BF16), vb)
            x[4:6] = [intra, xa]
            x.append(xb)
        states = [sst[0], sst[1]]
        for x in items:
            d, rows, _, qg, intra, xa, ea, eb, xb = x
            s0 = states[d]
            if d == 0:
                o_a = _dot(qg, s0.astype(BF16))
                s1 = s0 * ea + xa
                o_b = _dot(qg, s1.astype(BF16))
                s2 = s1 * eb + xb
            else:
                o_b = _dot(qg, s0.astype(BF16))
                s1 = s0 * eb + xb
                o_a = _dot(qg, s1.astype(BF16))
                s2 = s1 * ea + xa
            states[d] = s2
            osum[rows, :] += jnp.where(first_row_v, o_a, o_b) + intra
        sst[0] = states[0]
        sst[1] = states[1]
        return 0

    lax.fori_loop(0, npair // unroll, pairs, 0)

    o = osum[...]
    y = o * lax.rsqrt(jnp.mean(o * o, axis=-1, keepdims=True) + NORM_EPS) * nw_ref[...]
    o_ref[...] = (y * _silu(g_ref[...].astype(F32))).astype(BF16)


def _gla(pbig, gc, w2f, w2b, b2f, b2b, norm_w, *, unroll=4):
    b, t, _ = pbig.shape
    dk = GLA_HEAD_K
    dv = GLA_HEAD_V
    return pl.pallas_call(
        functools.partial(_gla_kernel, seq=t, unroll=unroll),
        grid=(b, GLA_HEADS),
        in_specs=[
            pl.BlockSpec((None, t, dk), lambda i, j: (i, 0, COL_QB // dk + j)),
            pl.BlockSpec((None, t, dk), lambda i, j: (i, 0, COL_KB // dk + j)),
            pl.BlockSpec((None, t, dv), lambda i, j: (i, 0, COL_VB // dv + j)),
            pl.BlockSpec((None, t, dv), lambda i, j: (i, 0, COL_GB // dv + j)),
            pl.BlockSpec((None, t, LANES), lambda i, j: (i, 0, 0)),
            pl.BlockSpec((LANES, dk), lambda i, j: (0, j)),
            pl.BlockSpec((LANES, dk), lambda i, j: (0, j)),
            pl.BlockSpec((1, dk), lambda i, j: (0, j)),
            pl.BlockSpec((1, dk), lambda i, j: (0, j)),
            pl.BlockSpec((1, dv), lambda i, j: (0, 0)),
        ],
        out_specs=pl.BlockSpec((None, t, dv), lambda i, j: (i, 0, j)),
        out_shape=jax.ShapeDtypeStruct((b, t, GLA_VAL_DIM), BF16),
        scratch_shapes=[
            pltpu.VMEM((2, 2 * GLA_CHUNK, 2 * GLA_CHUNK), BF16),
            pltpu.VMEM((2, dk, dv), F32),
            pltpu.VMEM((t, dv), F32),
        ],
        compiler_params=pltpu.CompilerParams(
            dimension_semantics=("arbitrary", "arbitrary"), vmem_limit_bytes=VMEM_LIMIT),
        name="gla",
    )(pbig, pbig, pbig, pbig, gc, w2f, w2b, b2f, b2b, norm_w)


def _merge_kernel(oa_ref, ob_ref, ga_ref, gb_ref, x_ref, wa_ref, wb_ref, wo_ref, lnw_ref, out_ref):
    ya = _dot(oa_ref[...], wa_ref[...])
    yb = _dot(ob_ref[...], wb_ref[...])
    merged = (_sigmoid(ga_ref[...].astype(F32)) * ya + _sigmoid(gb_ref[...].astype(F32)) * yb)
    out = _dot(merged.astype(BF16), wo_ref[...])
    y = out * lax.rsqrt(jnp.mean(out * out, axis=-1, keepdims=True) + NORM_EPS) * lnw_ref[...]
    out_ref[...] = x_ref[...] + y


def _merge(oa, ob, pbig2d, x2d, wa, wb, wo, ln_w, *, tm=512):
    m, d = x2d.shape
    return pl.pallas_call(
        _merge_kernel,
        grid=(m // tm,),
        in_specs=[
            pl.BlockSpec((tm, d), lambda i: (i, 0)),
            pl.BlockSpec((tm, d), lambda i: (i, 0)),
            pl.BlockSpec((tm, d), lambda i: (i, COL_GATE_A // D_MODEL)),
            pl.BlockSpec((tm, d), lambda i: (i, COL_GATE_B // D_MODEL)),
            pl.BlockSpec((tm, d), lambda i: (i, 0)),
            pl.BlockSpec((d, d), lambda i: (0, 0)),
            pl.BlockSpec((d, d), lambda i: (0, 0)),
            pl.BlockSpec((d, d), lambda i: (0, 0)),
            pl.BlockSpec((1, d), lambda i: (0, 0)),
        ],
        out_specs=pl.BlockSpec((tm, d), lambda i: (i, 0)),
        out_shape=jax.ShapeDtypeStruct((m, d), F32),
        compiler_params=pltpu.CompilerParams(
            dimension_semantics=("arbitrary",), vmem_limit_bytes=VMEM_LIMIT),
        name="merge",
    )(oa, ob, pbig2d, pbig2d, x2d, wa, wb, wo, ln_w)


def _layer(x, ln_pre_w, w_in, conv_w, a_log_fwd, a_log_bwd, dt_bias_fwd, dt_bias_bwd,
           gdn_norm_w, w_proj_gdn, gk_w2_fwd, gk_b2_fwd, gk_w2_bwd, gk_b2_bwd,
           gla_norm_w, w_proj_gla, w_out, ln_post_w):
    b, t, d = x.shape
    m = b * t
    o = _OFF
    wb = w_in.astype(BF16)
    w_big = jnp.concatenate([wb[:, o[0]:o[2]], wb[:, o[6]:o[10]], wb[:, o[12]:o[14]]], axis=1)
    w_small = jnp.concatenate(
        [wb[:, o[2]:o[6]], wb[:, o[10]:o[12]],
         jnp.zeros((d, LANES - 4 * GDN_HEADS - 2 * GLA_GATE_RANK), BF16)], axis=1)
    pad = jnp.zeros((LANES - 2 * GDN_HEADS,), F32)
    alog_vec = jnp.concatenate([a_log_fwd, a_log_bwd, pad]).reshape(1, LANES)
    dtb_vec = jnp.concatenate([dt_bias_fwd, dt_bias_bwd, pad]).reshape(1, LANES)

    def w2_ext(w2, lane0):
        z = jnp.zeros((LANES, GLA_KEY_DIM), F32)
        return z.at[lane0:lane0 + GLA_GATE_RANK, :].set(w2).astype(BF16)

    x2d = x.reshape(m, d)
    pbig, psmall = _inproj(x2d, ln_pre_w.reshape(1, d), w_big, w_small)
    gc, gct = _gates(psmall.reshape(b, t, LANES), alog_vec, dtb_vec)
    pbig3 = pbig.reshape(b, t, N_BIG)
    oa = _gdn(pbig3, gc, gct, conv_w, gdn_norm_w.reshape(1, GDN_HEAD_DIM))
    ob = _gla(pbig3, gc, w2_ext(gk_w2_fwd, LANE_RF), w2_ext(gk_w2_bwd, LANE_RB),
              gk_b2_fwd.reshape(1, GLA_KEY_DIM), gk_b2_bwd.reshape(1, GLA_KEY_DIM),
              gla_norm_w.reshape(1, GLA_HEAD_V))
    out = _merge(oa.reshape(m, GDN_WIDTH), ob.reshape(m, GLA_VAL_DIM), pbig, x2d,
                 w_proj_gdn.astype(BF16), w_proj_gla.astype(BF16), w_out.astype(BF16),
                 ln_post_w.reshape(1, d))
    return out.reshape(b, t, d)


def kernel(x, ln_pre_w, w_in, conv_w, a_log_fwd, a_log_bwd, dt_bias_fwd, dt_bias_bwd, gdn_norm_w,
           w_proj_gdn, gk_w2_fwd, gk_b2_fwd, gk_w2_bwd, gk_b2_bwd, gla_norm_w, w_proj_gla, w_out,
           ln_post_w):
    h = x
    for l in range(ln_pre_w.shape[0]):
        h = _layer(h, ln_pre_w[l], w_in[l], conv_w[l], a_log_fwd[l], a_log_bwd[l],
                   dt_bias_fwd[l], dt_bias_bwd[l], gdn_norm_w[l], w_proj_gdn[l],
                   gk_w2_fwd[l], gk_b2_fwd[l], gk_w2_bwd[l], gk_b2_bwd[l],
                   gla_norm_w[l], w_proj_gla[l], w_out[l], ln_post_w[l])
    return h
```

```python
import functools

import jax
import jax.numpy as jnp
from jax import lax
from jax.experimental import pallas as pl
from jax.experimental.pallas import tpu as pltpu

F32 = jnp.float32
BF16 = jnp.bfloat16

D_MODEL = 1024
GDN_HEADS = 8
GDN_HEAD_DIM = 128
GDN_WIDTH = GDN_HEADS * GDN_HEAD_DIM
CONV_K = 5
GLA_HEADS = 4
GLA_KEY_DIM = D_MODEL // 2
GLA_VAL_DIM = D_MODEL
GLA_HEAD_K = GLA_KEY_DIM // GLA_HEADS
GLA_HEAD_V = GLA_VAL_DIM // GLA_HEADS
GLA_GATE_RANK = 16
GLA_GATE_NORMALIZER = 16.0
GLA_CHUNK = 64
NORM_EPS = 1e-6
MASKED_LOG = -1e30

LANES = 128
GDN_CHUNK = LANES
VMEM_LIMIT = 56 * 1024 * 1024

_IN_SIZES = [3 * GDN_WIDTH, GDN_WIDTH, GDN_HEADS, GDN_HEADS, GDN_HEADS, GDN_HEADS,
             GLA_KEY_DIM, GLA_KEY_DIM, GLA_VAL_DIM, GLA_VAL_DIM,
             GLA_GATE_RANK, GLA_GATE_RANK, D_MODEL, D_MODEL]
_OFF = [0]
for _s in _IN_SIZES:
    _OFF.append(_OFF[-1] + _s)
N_BIG = 3 * GDN_WIDTH + GDN_WIDTH + 2 * GLA_KEY_DIM + 2 * GLA_VAL_DIM + 2 * D_MODEL
COL_Z = 3 * GDN_WIDTH
COL_QB = COL_Z + GDN_WIDTH
COL_KB = COL_QB + GLA_KEY_DIM
COL_VB = COL_KB + GLA_KEY_DIM
COL_GB = COL_VB + GLA_VAL_DIM
COL_GATE_A = COL_GB + GLA_VAL_DIM
COL_GATE_B = COL_GATE_A + D_MODEL
LANE_GF, LANE_GB, LANE_BF, LANE_BB = 0, 8, 16, 24
LANE_RF, LANE_RB = 32, 48


def _dot(a, b):
    return jnp.dot(a, b, preferred_element_type=F32)


def _dot_nt(a, b):
    return lax.dot_general(a, b, (((1,), (1,)), ((), ())), preferred_element_type=F32)


def _split2(x):
    hi = x.astype(BF16)
    lo = (x - hi.astype(F32)).astype(BF16)
    return hi, lo


def _split3(x):
    hi = x.astype(BF16)
    r = x - hi.astype(F32)
    mid = r.astype(BF16)
    lo = (r - mid.astype(F32)).astype(BF16)
    return hi, mid, lo


def _sigmoid(x):
    return jax.nn.sigmoid(x)


def _silu(x):
    return x * _sigmoid(x)


def _softplus(x):
    return jnp.maximum(x, 0.0) + jnp.log(1.0 + jnp.exp(-jnp.abs(x)))


def _log_sigmoid(x):
    return jnp.minimum(x, 0.0) - jnp.log(1.0 + jnp.exp(-jnp.abs(x)))


def _inproj_kernel(x_ref, lnw_ref, wbig_ref, wsmall_ref, pbig_ref, psmall_ref, h_ref, *, tm):
    j = pl.program_id(1)

    @pl.when(j == 0)
    def _():
        def body(r, _):
            rows = pl.ds(pl.multiple_of(r * LANES, LANES), LANES)
            x = x_ref[rows, :]
            ms = jnp.mean(x * x, axis=-1, keepdims=True)
            h_ref[rows, :] = (x * lax.rsqrt(ms + NORM_EPS) * lnw_ref[...]).astype(BF16)
            return 0
        lax.fori_loop(0, tm // LANES, body, 0)
        psmall_ref[...] = _dot_nt(h_ref[...], wsmall_ref[...])

    pbig_ref[...] = _dot_nt(h_ref[...], wbig_ref[...]).astype(BF16)


def _inproj(x2d, ln_w, w_big, w_small, *, tm=1024, tn=2304):
    m, d = x2d.shape
    return pl.pallas_call(
        functools.partial(_inproj_kernel, tm=tm),
        grid=(m // tm, N_BIG // tn),
        in_specs=[
            pl.BlockSpec((tm, d), lambda i, j: (i, 0)),
            pl.BlockSpec((1, d), lambda i, j: (0, 0)),
            pl.BlockSpec((tn, d), lambda i, j: (j, 0)),
            pl.BlockSpec((LANES, d), lambda i, j: (0, 0)),
        ],
        out_specs=[
            pl.BlockSpec((tm, tn), lambda i, j: (i, j)),
            pl.BlockSpec((tm, LANES), lambda i, j: (i, 0)),
        ],
        out_shape=[
            jax.ShapeDtypeStruct((m, N_BIG), BF16),
            jax.ShapeDtypeStruct((m, LANES), F32),
        ],
        scratch_shapes=[pltpu.VMEM((tm, d), BF16)],
        compiler_params=pltpu.CompilerParams(
            dimension_semantics=("arbitrary", "arbitrary"),
            vmem_limit_bytes=VMEM_LIMIT),
        name="inproj",
    )(x2d, ln_w, w_big, w_small)


def _gates_kernel(ps_ref, alog_ref, dtb_ref, out_ref, out_t_ref, *, seq):
    c = GDN_CHUNK
    ri = lax.broadcasted_iota(jnp.int32, (c, c), 0)
    ci = lax.broadcasted_iota(jnp.int32, (c, c), 1)
    tri_l = jnp.where(ri >= ci, 1.0, 0.0).astype(BF16)
    tri_u = jnp.where(ri <= ci, 1.0, 0.0).astype(BF16)
    lane = lax.broadcasted_iota(jnp.int32, (c, LANES), 1)
    neg_a = -jnp.exp(alog_ref[...])

    for n in range(seq // c):
        a = ps_ref[n * c:(n + 1) * c, :]
        g = neg_a * _softplus(a + dtb_ref[...])
        beta = _sigmoid(a)
        parts = jnp.concatenate(_split3(g), axis=1)
        cl = _dot(tri_l, parts)
        cu = _dot(tri_u, parts)
        cl = cl[:, :LANES] + cl[:, LANES:2 * LANES] + cl[:, 2 * LANES:]
        cu = cu[:, :LANES] + cu[:, LANES:2 * LANES] + cu[:, 2 * LANES:]
        res = jnp.where(
            lane < LANE_GB, cl, jnp.where(lane < LANE_BF, cu, jnp.where(lane < LANE_RF, beta, a)))
        out_ref[n * c:(n + 1) * c, :] = jnp.concatenate(_split2(res), axis=1)
        out_t_ref[:, n * c:(n + 1) * c] = res.T


def _gates(ps, alog_vec, dtb_vec):
    b, t, _ = ps.shape
    return pl.pallas_call(
        functools.partial(_gates_kernel, seq=t),
        grid=(b,),
        in_specs=[
            pl.BlockSpec((None, t, LANES), lambda i: (i, 0, 0)),
            pl.BlockSpec((1, LANES), lambda i: (0, 0)),
            pl.BlockSpec((1, LANES), lambda i: (0, 0)),
        ],
        out_specs=[
            pl.BlockSpec((None, t, 2 * LANES), lambda i: (i, 0, 0)),
            pl.BlockSpec((None, LANES, t), lambda i: (i, 0, 0)),
        ],
        out_shape=[
            jax.ShapeDtypeStruct((b, t, 2 * LANES), BF16),
            jax.ShapeDtypeStruct((b, LANES, t), F32),
        ],
        compiler_params=pltpu.CompilerParams(
            dimension_semantics=("arbitrary",), vmem_limit_bytes=VMEM_LIMIT),
        name="gates",
    )(ps, alog_vec, dtb_vec)


def _block_diag(xp):
    c = xp.shape[0]
    z = jnp.zeros((c, c), xp.dtype)
    return jnp.concatenate([jnp.concatenate([xp[:, :c], z], axis=1),
                            jnp.concatenate([z, xp[:, c:]], axis=1)], axis=0)


def _odd_blocks(x, s):
    return jnp.concatenate([x[b * s:(b + 1) * s] for b in range(1, x.shape[0] // s, 2)], axis=0)


def _spread_odd(x_odd, s):
    z = jnp.zeros((s, x_odd.shape[1]), x_odd.dtype)
    parts = []
    for j in range(x_odd.shape[0] // s):
        parts += [z, x_odd[j * s:(j + 1) * s]]
    return jnp.concatenate(parts, axis=0)


def _sub_from_odd(x, r_odd, s):
    parts = []
    for b in range(x.shape[0] // s):
        blk = x[b * s:(b + 1) * s]
        if b % 2:
            blk = blk - r_odd[(b // 2) * s:(b // 2 + 1) * s]
        parts.append(blk)
    return jnp.concatenate(parts, axis=0)


GDN_FULL_LEVELS = 3
GDN_BF16_ROWS = 16


def _unit_tri_inverse_stages(lps, masks_ref, mbf_ref, modd_ref, moddbf_ref, eye_p, out):
    ts = [eye_p - lp * masks_ref[...] for lp in lps]
    lbs = [lp.astype(BF16) for lp in lps]
    for lvl in range(1, 7):
        s = 1 << lvl
        tbs = [t.astype(BF16) for t in ts]
        if lvl < GDN_FULL_LEVELS:
            ms = [_dot(lb * mbf_ref[lvl - 1], _block_diag(tb)) for lb, tb in zip(lbs, tbs)]
            yield
            ts = [t - _dot(tb, _block_diag(m.astype(BF16))) for t, tb, m in zip(ts, tbs, ms)]
        else:
            if s < GDN_BF16_ROWS:
                cs = [(_odd_blocks(lp, s) * modd_ref[...]).astype(BF16) for lp in lps]
            else:
                mo = moddbf_ref[lvl - GDN_FULL_LEVELS - 1]
                cs = [_odd_blocks(lb, s) * mo for lb in lbs]
            ms = [_dot(cm, _block_diag(tb)) for cm, tb in zip(cs, tbs)]
            yield
            ts = [_sub_from_odd(t, _dot(_odd_blocks(t, s).astype(BF16),
                                        _block_diag(_spread_odd(m, s).astype(BF16))), s)
                  for t, m in zip(ts, ms)]
        yield
    out.extend(ts)


def _gdn_kernel(q_ref, k_ref, v_ref, z_ref, gch_ref, gct_ref, cwq_ref, cwk_ref, cwv_ref, nw_ref,
                o_ref, xpq, xpk, xpv, qs, ks, vs, kks, qks, aqs, bos, egls, osum, masks, mbf, modd,
                moddbf, selc, rowbuf, *, seq, unroll, stagger):
    c = GDN_CHUNK
    nc = seq // c
    h = pl.program_id(1)
    ri = lax.broadcasted_iota(jnp.int32, (c, c), 0)
    ci = lax.broadcasted_iota(jnp.int32, (c, c), 1)
    lower, upper = ri >= ci, ri <= ci
    rp = lax.broadcasted_iota(jnp.int32, (c, 2 * c), 0)
    cp = lax.broadcasted_iota(jnp.int32, (c, 2 * c), 1) & (c - 1)
    eye_p = jnp.where(rp == cp, 1.0, 0.0)

    for lvl in range(7):
        m = jnp.where(((rp >> (lvl + 1)) == (cp >> (lvl + 1))) & ((rp >> lvl) != (cp >> lvl)), 1.0, 0.0)
        if lvl == 0:
            masks[...] = m
        elif lvl < GDN_FULL_LEVELS:
            mbf[lvl - 1] = m.astype(BF16)
        elif (1 << lvl) < GDN_BF16_ROWS:
            modd[...] = _odd_blocks(m, 1 << lvl)
        else:
            moddbf[lvl - GDN_FULL_LEVELS - 1] = _odd_blocks(m, 1 << lvl).astype(BF16)

    l2 = lax.broadcasted_iota(jnp.int32, (2 * c, 2 * c), 0) & (c - 1)
    cb = lax.broadcasted_iota(jnp.int32, (2 * c, 2 * c), 1) >> 7
    lanes_g = (h + LANE_GF, h + LANE_GB)
    lanes_b = (h + LANE_BF, h + LANE_BB)
    for d in range(2):
        selc[d] = jnp.where(l2 == jnp.where(cb == 0, lanes_g[d], lanes_b[d]), 1.0, 0.0).astype(BF16)

    sub = lax.broadcasted_iota(jnp.int32, (8, seq), 0)
    for idx, lane0 in enumerate((LANE_GF, LANE_GB, LANE_BB)):
        rowbuf[idx:idx + 1, :] = jnp.sum(
            jnp.where(sub == h, gct_ref[lane0:lane0 + 8, :], 0.0), axis=0, keepdims=True)

    zeros8 = jnp.zeros((8, LANES), F32)
    rb = 2 * c
    nrb = seq // rb

    def conv_fill(x_ref, xp):
        xp[pl.ds(0, 8), :] = zeros8
        xp[pl.ds(seq + 8, 8), :] = zeros8
        xp[pl.ds(8, seq), :] = x_ref[...].astype(F32)

    def conv_silu(xp, cw_ref, r):
        base = 8 - CONV_K // 2 + r * rb
        acc = cw_ref[0:1, :] * xp[pl.ds(base, rb), :]
        for j in range(1, CONV_K):
            acc = acc + cw_ref[j:j + 1, :] * xp[pl.ds(base + j, rb), :]
        return _silu(acc)

    def l2norm(y):
        return y * lax.rsqrt(jnp.sum(y * y, axis=-1, keepdims=True) + NORM_EPS)

    conv_fill(q_ref, xpq)
    conv_fill(k_ref, xpk)
    conv_fill(v_ref, xpv)
    for r in range(nrb):
        qs[r * rb:(r + 1) * rb, :] = l2norm(conv_silu(xpq, cwq_ref, r)) * (GDN_HEAD_DIM ** -0.5)
        ks[r * rb:(r + 1) * rb, :] = l2norm(conv_silu(xpk, cwk_ref, r))
        for n in range(r * rb // c, (r + 1) * rb // c):
            kcb = ks[n * c:(n + 1) * c, :].astype(BF16)
            kks[n] = _dot_nt(kcb, kcb)
            qks[n] = _dot_nt(qs[n * c:(n + 1) * c, :].astype(BF16), kcb)
    osum[...] = jnp.zeros((seq, LANES), F32)
    v_pending = list(range(nrb))

    def conv_v_block():
        r = v_pending.pop(0)
        vs[r * rb:(r + 1) * rb, :] = conv_silu(xpv, cwv_ref, r)

    def row_form(idx, n):
        return jnp.broadcast_to(rowbuf[idx:idx + 1, n * c:(n + 1) * c], (c, c))

    def prep(trip):
        pairs = [(trip * unroll + u, nc - 1 - trip * unroll - u) for u in range(unroll)]
        work, lps = [], []
        for pair in pairs:
            item = []
            for d, n in enumerate(pair):
                sel = _dot(gch_ref[n * c:(n + 1) * c, :], selc[d])
                item.append([n, sel[:, :c], sel[:, c:]])
            fw, bw = item
            n, gcol, beta = fw
            decay = jnp.exp(jnp.where(lower, gcol - row_form(0, n), MASKED_LOG))
            l_f = kks[n] * decay * beta
            fw.append(qks[n] * decay)
            n, gcol, beta = bw
            grow = row_form(1, n)
            decay = jnp.exp(jnp.where(upper, gcol - grow, MASKED_LOG))
            l_b = kks[n] * jnp.exp(jnp.where(lower, grow - gcol, MASKED_LOG)) * row_form(2, n)
            bw.append(qks[n] * decay)
            lps.append(jnp.concatenate([l_f, l_b], axis=1))
            work.append(item)
            yield
        tps = []
        yield from _unit_tri_inverse_stages(lps, masks, mbf, modd, moddbf, eye_p, tps)
        assert not v_pending
        sols = []
        for item, tp in zip(work, tps):
            for d, (n, gcol, beta, attn) in enumerate(item):
                t_inv = tp[:, :c] if d == 0 else tp[:, c:].T
                kc = ks[n * c:(n + 1) * c, :]
                eg = jnp.exp(gcol)
                rhs = jnp.concatenate([(vs[n * c:(n + 1) * c, :] * beta).astype(BF16),
                                       (kc * beta * eg).astype(BF16)], axis=1)
                sols.append(_dot(t_inv.astype(BF16), rhs))
        yield
        k = 0
        for item in work:
            for d, (n, gcol, beta, attn) in enumerate(item):
                kc = ks[n * c:(n + 1) * c, :]
                glast = gcol[c - 1:c, :] if d == 0 else gcol[0:1, :]
                kdt = (kc * jnp.exp(glast - gcol)).T
                lhs = jnp.concatenate([kdt.astype(BF16), attn.astype(BF16)], axis=0)
                z = _dot(lhs, sols[k].astype(BF16))
                k += 1
                qd = qs[n * c:(n + 1) * c, :] * jnp.exp(gcol)
                aqs[d, n, 0:c, :] = (-z[:c, c:]).astype(BF16)
                aqs[d, n, c:2 * c, :] = (qd - z[c:, c:]).astype(BF16)
                bos[d, n] = z[:, :c]
                egls[d, n * 8:(n + 1) * 8, :] = jnp.broadcast_to(jnp.exp(glast), (8, LANES))
        yield

    states = [jnp.zeros((GDN_HEAD_DIM, GDN_HEAD_DIM), F32)] * 2

    def scan_step(trip, u):
        for d in range(2):
            n = trip * unroll + u if d == 0 else nc - 1 - trip * unroll - u
            r = _dot(aqs[d, n], states[d].astype(BF16)) + bos[d, n]
            states[d] = states[d] * egls[d, n * 8:n * 8 + 1, :] + r[:c]
            osum[n * c:(n + 1) * c, :] += r[c:]

    trips = nc // unroll
    active, scan_queue, started, tick = {}, [], 0, 0
    while started < trips or active or scan_queue:
        if started < trips and tick >= started * stagger:
            active[started] = prep(started)
            started += 1
        for trip in sorted(active):
            if next(active[trip], "done") == "done":
                del active[trip]
                scan_queue.extend((trip, u) for u in range(unroll))
        if v_pending:
            conv_v_block()
        if scan_queue:
            scan_step(*scan_queue.pop(0))
        tick += 1

    o = osum[...]
    y = o * lax.rsqrt(jnp.mean(o * o, axis=-1, keepdims=True) + NORM_EPS) * nw_ref[...]
    o_ref[...] = (y * _silu(z_ref[...].astype(F32))).astype(BF16)


def _gdn(pbig, gch, gct, conv_w, norm_w, *, unroll=4, stagger=5):
    b, t, _ = pbig.shape
    c = GDN_CHUNK
    nc = t // c
    hd = GDN_HEAD_DIM
    nh = GDN_HEADS
    assert nc % unroll == 0
    return pl.pallas_call(
        functools.partial(_gdn_kernel, seq=t, unroll=unroll, stagger=stagger),
        grid=(b, nh),
        in_specs=[
            pl.BlockSpec((None, t, hd), lambda i, j: (i, 0, j)),
            pl.BlockSpec((None, t, hd), lambda i, j: (i, 0, nh + j)),
            pl.BlockSpec((None, t, hd), lambda i, j: (i, 0, 2 * nh + j)),
            pl.BlockSpec((None, t, hd), lambda i, j: (i, 0, COL_Z // hd + j)),
            pl.BlockSpec((None, t, 2 * LANES), lambda i, j: (i, 0, 0)),
            pl.BlockSpec((None, LANES, t), lambda i, j: (i, 0, 0)),
            pl.BlockSpec((CONV_K, hd), lambda i, j: (0, j)),
            pl.BlockSpec((CONV_K, hd), lambda i, j: (0, nh + j)),
            pl.BlockSpec((CONV_K, hd), lambda i, j: (0, 2 * nh + j)),
            pl.BlockSpec((1, hd), lambda i, j: (0, 0)),
        ],
        out_specs=pl.BlockSpec((None, t, hd), lambda i, j: (i, 0, j)),
        out_shape=jax.ShapeDtypeStruct((b, t, GDN_WIDTH), BF16),
        scratch_shapes=[
            pltpu.VMEM((t + 16, hd), F32),
            pltpu.VMEM((t + 16, hd), F32),
            pltpu.VMEM((t + 16, hd), F32),
            pltpu.VMEM((t, hd), F32),
            pltpu.VMEM((t, hd), F32),
            pltpu.VMEM((t, hd), F32),
            pltpu.VMEM((nc, c, c), F32),
            pltpu.VMEM((nc, c, c), F32),
            pltpu.VMEM((2, nc, 2 * c, hd), BF16),
            pltpu.VMEM((2, nc, 2 * c, hd), F32),
            pltpu.VMEM((2, nc * 8, LANES), F32),
            pltpu.VMEM((t, hd), F32),
            pltpu.VMEM((c, 2 * c), F32),
            pltpu.VMEM((GDN_FULL_LEVELS - 1, c, 2 * c), BF16),
            pltpu.VMEM((c // 2, 2 * c), F32),
            pltpu.VMEM((3, c // 2, 2 * c), BF16),
            pltpu.VMEM((2, 2 * c, 2 * c), BF16),
            pltpu.VMEM((8, t), F32),
        ],
        compiler_params=pltpu.CompilerParams(
            dimension_semantics=("arbitrary", "arbitrary"), vmem_limit_bytes=VMEM_LIMIT),
        name="gdn",
    )(pbig, pbig, pbig, pbig, gch, gct, conv_w, conv_w, conv_w, norm_w)


def _gla_kernel(q_ref, k_ref, v_ref, g_ref, gc_ref, w2f_ref, w2b_ref, b2f_ref, b2b_ref, nw_ref,
                o_ref, tris, sst, osum, *, seq, unroll):
    c2 = 2 * GLA_CHUNK
    npair = seq // c2
    dk = GLA_HEAD_K
    dv = GLA_HEAD_V
    ri = lax.broadcasted_iota(jnp.int32, (c2, c2), 0)
    ci = lax.broadcasted_iota(jnp.int32, (c2, c2), 1)
    same = (ri >> 6) == (ci >> 6)
    incls = [same & (ri >= ci), same & (ri <= ci)]
    first_row = lax.broadcasted_iota(jnp.int32, (c2, dk), 0) < GLA_CHUNK
    first_row_v = lax.broadcasted_iota(jnp.int32, (c2, dv), 0) < GLA_CHUNK
    first_lane = lax.broadcasted_iota(jnp.int32, (dk, c2), 1) < GLA_CHUNK
    w2_refs = (w2f_ref, w2b_ref)
    b2_refs = (b2f_ref, b2b_ref)

    for d in range(2):
        tris[d] = jnp.where(incls[d], 1.0, 0.0).astype(BF16)
    osum[...] = jnp.zeros((seq, dv), F32)
    sst[...] = jnp.zeros((2, dk, dv), F32)

    def pairs(it, _):
        items = []
        for u in range(unroll):
            for d in range(2):
                pp = it * unroll + u
                pp = pp if d == 0 else npair - 1 - pp
                rows = pl.ds(pl.multiple_of(pp * c2, c2), c2)
                pre = _dot(gc_ref[rows, :], w2_refs[d][...]) + b2_refs[d][...]
                items.append([d, rows, pre])
        for x in items:
            gk = _log_sigmoid(x[2]) / GLA_GATE_NORMALIZER
            cs = _dot(tris[x[0]], jnp.concatenate(_split3(gk), axis=1))
            x[2] = cs[:, :dk] + cs[:, dk:2 * dk] + cs[:, 2 * dk:]
        for x in items:
            d, rows, gcum = x
            q = q_ref[rows, :].astype(F32) * (dk ** -0.5)
            k = k_ref[rows, :].astype(F32)
            qg = (q * jnp.exp(gcum)).astype(BF16)
            kg = (k * jnp.exp(-gcum)).astype(BF16)
            attn = jnp.where(incls[d], _dot_nt(qg, kg), 0.0)
            if d == 0:
                last_a, last_b = gcum[GLA_CHUNK - 1:GLA_CHUNK, :], gcum[c2 - 1:c2, :]
            else:
                last_a, last_b = gcum[0:1, :], gcum[GLA_CHUNK:GLA_CHUNK + 1, :]
            glast = jnp.where(first_row, last_a, last_b)
            kdt = (k * jnp.exp(glast - gcum)).T
            gt = gcum.T
            if d == 0:
                ea = jnp.exp(gt[:, GLA_CHUNK - 1:GLA_CHUNK])
                eb = jnp.exp(gt[:, c2 - 1:c2])
            else:
                ea = jnp.exp(gt[:, 0:1])
                eb = jnp.exp(gt[:, GLA_CHUNK:GLA_CHUNK + 1])
            x.extend([qg, attn, kdt, ea, eb])
        for x in items:
            d, rows, _, qg, attn, kdt, ea, eb = x
            vb = v_ref[rows, :]
            intra = _dot(attn.astype(BF16), vb)
            xa = _dot(jnp.where(first_lane, kdt, 0.0).astype(BF16), vb)
            xb = _dot(jnp.where(first_lane, 0.0, kdt).astype(BF16), vb)
            x[4:6] = [intra, xa]
            x.append(xb)
        states = [sst[0], sst[1]]
        for x in items:
            d, rows, _, qg, intra, xa, ea, eb, xb = x
            s0 = states[d]
            if d == 0:
                o_a = _dot(qg, s0.astype(BF16))
                s1 = s0 * ea + xa
                o_b = _dot(qg, s1.astype(BF16))
                s2 = s1 * eb + xb
            else:
                o_b = _dot(qg, s0.astype(BF16))
                s1 = s0 * eb + xb
                o_a = _dot(qg, s1.astype(BF16))
                s2 = s1 * ea + xa
            states[d] = s2
            osum[rows, :] += jnp.where(first_row_v, o_a, o_b) + intra
        sst[0] = states[0]
        sst[1] = states[1]
        return 0

    lax.fori_loop(0, npair // unroll, pairs, 0)

    o = osum[...]
    y = o * lax.rsqrt(jnp.mean(o * o, axis=-1, keepdims=True) + NORM_EPS) * nw_ref[...]
    o_ref[...] = (y * _silu(g_ref[...].astype(F32))).astype(BF16)


def _gla(pbig, gc, w2f, w2b, b2f, b2b, norm_w, *, unroll=4):
    b, t, _ = pbig.shape
    dk = GLA_HEAD_K
    dv = GLA_HEAD_V
    return pl.pallas_call(
        functools.partial(_gla_kernel, seq=t, unroll=unroll),
        grid=(b, GLA_HEADS),
        in_specs=[
            pl.BlockSpec((None, t, dk), lambda i, j: (i, 0, COL_QB // dk + j)),
            pl.BlockSpec((None, t, dk), lambda i, j: (i, 0, COL_KB // dk + j)),
            pl.BlockSpec((None, t, dv), lambda i, j: (i, 0, COL_VB // dv + j)),
            pl.BlockSpec((None, t, dv), lambda i, j: (i, 0, COL_GB // dv + j)),
            pl.BlockSpec((None, t, LANES), lambda i, j: (i, 0, 0)),
            pl.BlockSpec((LANES, dk), lambda i, j: (0, j)),
            pl.BlockSpec((LANES, dk), lambda i, j: (0, j)),
            pl.BlockSpec((1, dk), lambda i, j: (0, j)),
            pl.BlockSpec((1, dk), lambda i, j: (0, j)),
            pl.BlockSpec((1, dv), lambda i, j: (0, 0)),
        ],
        out_specs=pl.BlockSpec((None, t, dv), lambda i, j: (i, 0, j)),
        out_shape=jax.ShapeDtypeStruct((b, t, GLA_VAL_DIM), BF16),
        scratch_shapes=[
            pltpu.VMEM((2, 2 * GLA_CHUNK, 2 * GLA_CHUNK), BF16),
            pltpu.VMEM((2, dk, dv), F32),
            pltpu.VMEM((t, dv), F32),
        ],
        compiler_params=pltpu.CompilerParams(
            dimension_semantics=("arbitrary", "arbitrary"), vmem_limit_bytes=VMEM_LIMIT),
        name="gla",
    )(pbig, pbig, pbig, pbig, gc, w2f, w2b, b2f, b2b, norm_w)


def _merge_kernel(oa_ref, ob_ref, ga_ref, gb_ref, x_ref, wa_ref, wb_ref, wo_ref, lnw_ref, out_ref):
    ya = _dot(oa_ref[...], wa_ref[...])
    yb = _dot(ob_ref[...], wb_ref[...])
    merged = (_sigmoid(ga_ref[...].astype(F32)) * ya + _sigmoid(gb_ref[...].astype(F32)) * yb)
    out = _dot(merged.astype(BF16), wo_ref[...])
    y = out * lax.rsqrt(jnp.mean(out * out, axis=-1, keepdims=True) + NORM_EPS) * lnw_ref[...]
    out_ref[...] = x_ref[...] + y


def _merge(oa, ob, pbig2d, x2d, wa, wb, wo, ln_w, *, tm=512):
    m, d = x2d.shape
    return pl.pallas_call(
        _merge_kernel,
        grid=(m // tm,),
        in_specs=[
            pl.BlockSpec((tm, d), lambda i: (i, 0)),
            pl.BlockSpec((tm, d), lambda i: (i, 0)),
            pl.BlockSpec((tm, d), lambda i: (i, COL_GATE_A // D_MODEL)),
            pl.BlockSpec((tm, d), lambda i: (i, COL_GATE_B // D_MODEL)),
            pl.BlockSpec((tm, d), lambda i: (i, 0)),
            pl.BlockSpec((d, d), lambda i: (0, 0)),
            pl.BlockSpec((d, d), lambda i: (0, 0)),
            pl.BlockSpec((d, d), lambda i: (0, 0)),
            pl.BlockSpec((1, d), lambda i: (0, 0)),
        ],
        out_specs=pl.BlockSpec((tm, d), lambda i: (i, 0)),
        out_shape=jax.ShapeDtypeStruct((m, d), F32),
        compiler_params=pltpu.CompilerParams(
            dimension_semantics=("arbitrary",), vmem_limit_bytes=VMEM_LIMIT),
        name="merge",
    )(oa, ob, pbig2d, pbig2d, x2d, wa, wb, wo, ln_w)


def _layer(x, ln_pre_w, w_in, conv_w, a_log_fwd, a_log_bwd, dt_bias_fwd, dt_bias_bwd,
           gdn_norm_w, w_proj_gdn, gk_w2_fwd, gk_b2_fwd, gk_w2_bwd, gk_b2_bwd,
           gla_norm_w, w_proj_gla, w_out, ln_post_w):
    b, t, d = x.shape
    m = b * t
    o = _OFF
    wt = w_in.T
    w_big = jnp.concatenate([wt[o[0]:o[2]], wt[o[6]:o[10]], wt[o[12]:o[14]]], axis=0).astype(BF16)
    w_small = jnp.concatenate(
        [wt[o[2]:o[6]], wt[o[10]:o[12]],
         jnp.zeros((LANES - 4 * GDN_HEADS - 2 * GLA_GATE_RANK, d), w_in.dtype)], axis=0).astype(BF16)
    pad = jnp.zeros((LANES - 2 * GDN_HEADS,), F32)
    alog_vec = jnp.concatenate([a_log_fwd, a_log_bwd, pad]).reshape(1, LANES)
    dtb_vec = jnp.concatenate([dt_bias_fwd, dt_bias_bwd, pad]).reshape(1, LANES)

    def w2_ext(w2, lane0):
        z = jnp.zeros((LANES, GLA_KEY_DIM), F32)
        return z.at[lane0:lane0 + GLA_GATE_RANK, :].set(w2).astype(BF16)

    x2d = x.reshape(m, d)
    pbig, psmall = _inproj(x2d, ln_pre_w.reshape(1, d), w_big, w_small)
    gch, gct = _gates(psmall.reshape(b, t, LANES), alog_vec, dtb_vec)
    pbig3 = pbig.reshape(b, t, N_BIG)
    oa = _gdn(pbig3, gch, gct, conv_w, gdn_norm_w.reshape(1, GDN_HEAD_DIM))
    ob = _gla(pbig3, gch, w2_ext(gk_w2_fwd, LANE_RF), w2_ext(gk_w2_bwd, LANE_RB),
              gk_b2_fwd.reshape(1, GLA_KEY_DIM), gk_b2_bwd.reshape(1, GLA_KEY_DIM),
              gla_norm_w.reshape(1, GLA_HEAD_V))
    out = _merge(oa.reshape(m, GDN_WIDTH), ob.reshape(m, GLA_VAL_DIM), pbig, x2d,
                 w_proj_gdn.astype(BF16), w_proj_gla.astype(BF16), w_out.astype(BF16),
                 ln_post_w.reshape(1, d))
    return out.reshape(b, t, d)


def kernel(x, ln_pre_w, w_in, conv_w, a_log_fwd, a_log_bwd, dt_bias_fwd, dt_bias_bwd, gdn_norm_w,
           w_proj_gdn, gk_w2_fwd, gk_b2_fwd, gk_w2_bwd, gk_b2_bwd, gla_norm_w, w_proj_gla, w_out,
           ln_post_w):
    h = x
    for l in range(ln_pre_w.shape[0]):
        h = _layer(h, ln_pre_w[l], w_in[l], conv_w[l], a_log_fwd[l], a_log_bwd[l],
                   dt_bias_fwd[l], dt_bias_bwd[l], gdn_norm_w[l], w_proj_gdn[l],
                   gk_w2_fwd[l], gk_b2_fwd[l], gk_w2_bwd[l], gk_b2_bwd[l],
                   gla_norm_w[l], w_proj_gla[l], w_out[l], ln_post_w[l])
    return h
```

```python
import functools

import jax
import jax.numpy as jnp
from jax import lax
from jax.experimental import pallas as pl
from jax.experimental.pallas import tpu as pltpu

F32 = jnp.float32
BF16 = jnp.bfloat16

D_MODEL = 1024
GDN_HEADS = 8
GDN_HEAD_DIM = 128
GDN_WIDTH = GDN_HEADS * GDN_HEAD_DIM
CONV_K = 5
GLA_HEADS = 4
GLA_KEY_DIM = D_MODEL // 2
GLA_VAL_DIM = D_MODEL
GLA_HEAD_K = GLA_KEY_DIM // GLA_HEADS
GLA_HEAD_V = GLA_VAL_DIM // GLA_HEADS
GLA_GATE_RANK = 16
GLA_GATE_NORMALIZER = 16.0
GLA_CHUNK = 64
NORM_EPS = 1e-6
MASKED_LOG = -1e30

LANES = 128
GDN_CHUNK = LANES
VMEM_LIMIT = 56 * 1024 * 1024

_IN_SIZES = [3 * GDN_WIDTH, GDN_WIDTH, GDN_HEADS, GDN_HEADS, GDN_HEADS, GDN_HEADS,
             GLA_KEY_DIM, GLA_KEY_DIM, GLA_VAL_DIM, GLA_VAL_DIM,
             GLA_GATE_RANK, GLA_GATE_RANK, D_MODEL, D_MODEL]
_OFF = [0]
for _s in _IN_SIZES:
    _OFF.append(_OFF[-1] + _s)
N_BIG = 3 * GDN_WIDTH + GDN_WIDTH + 2 * GLA_KEY_DIM + 2 * GLA_VAL_DIM + 2 * D_MODEL
COL_Z = 3 * GDN_WIDTH
COL_QB = COL_Z + GDN_WIDTH
COL_KB = COL_QB + GLA_KEY_DIM
COL_VB = COL_KB + GLA_KEY_DIM
COL_GB = COL_VB + GLA_VAL_DIM
COL_GATE_A = COL_GB + GLA_VAL_DIM
COL_GATE_B = COL_GATE_A + D_MODEL
LANE_GF, LANE_GB, LANE_BF, LANE_BB = 0, 8, 16, 24
LANE_RF, LANE_RB = 32, 48


def _dot(a, b):
    return jnp.dot(a, b, preferred_element_type=F32)


def _dot_nt(a, b):
    return lax.dot_general(a, b, (((1,), (1,)), ((), ())), preferred_element_type=F32)


def _split2(x):
    hi = x.astype(BF16)
    lo = (x - hi.astype(F32)).astype(BF16)
    return hi, lo


def _split3(x):
    hi = x.astype(BF16)
    r = x - hi.astype(F32)
    mid = r.astype(BF16)
    lo = (r - mid.astype(F32)).astype(BF16)
    return hi, mid, lo


def _sigmoid(x):
    return jax.nn.sigmoid(x)


def _silu(x):
    return x * _sigmoid(x)


def _softplus(x):
    return jnp.maximum(x, 0.0) + jnp.log(1.0 + jnp.exp(-jnp.abs(x)))


def _log_sigmoid(x):
    return jnp.minimum(x, 0.0) - jnp.log(1.0 + jnp.exp(-jnp.abs(x)))


def _wide_tile_row(j, tn):
    unit = _OFF[6] - _OFF[2]
    assert _OFF[12] - _OFF[10] == unit and _OFF[2] % unit == 0 and _OFF[10] % unit == 0 and tn % unit == 0
    start = j * (tn // unit)
    for first_wide_col in (_OFF[2], _OFF[10] - unit):
        start = start + jnp.where(j * tn >= first_wide_col, 1, 0)
    return start * unit


def _wprep_kernel(w_ref, o_ref):
    o_ref[...] = w_ref[...].astype(BF16)


def _wprep(w_t, *, tn=1024):
    n, d = w_t.shape
    assert all(seg % tn == 0 for seg in (_OFF[2], _OFF[10] - _OFF[6], _OFF[14] - _OFF[12]))
    return pl.pallas_call(
        _wprep_kernel,
        grid=(N_BIG // tn,),
        in_specs=[pl.BlockSpec((pl.Element(tn), pl.Element(d)), lambda j: (_wide_tile_row(j, tn), 0))],
        out_specs=pl.BlockSpec((tn, d), lambda j: (j, 0)),
        out_shape=jax.ShapeDtypeStruct((N_BIG, d), BF16),
        compiler_params=pltpu.CompilerParams(
            dimension_semantics=("arbitrary",), vmem_limit_bytes=VMEM_LIMIT),
        name="wprep",
    )(w_t)


def _inproj_kernel(x_ref, lnw_ref, wbig_ref, wsmall_ref, pbig_ref, psmall_ref, h_ref, *, tm):
    j = pl.program_id(1)

    @pl.when(j == 0)
    def _():
        def body(r, _):
            rows = pl.ds(pl.multiple_of(r * LANES, LANES), LANES)
            x = x_ref[rows, :]
            ms = jnp.mean(x * x, axis=-1, keepdims=True)
            h_ref[rows, :] = (x * lax.rsqrt(ms + NORM_EPS) * lnw_ref[...]).astype(BF16)
            return 0
        lax.fori_loop(0, tm // LANES, body, 0)
        psmall_ref[...] = _dot_nt(h_ref[...], wsmall_ref[...])

    pbig_ref[...] = _dot_nt(h_ref[...], wbig_ref[...]).astype(BF16)


def _inproj(x2d, ln_w, w_big, w_small, *, tm=1024, tn=2304):
    m, d = x2d.shape
    return pl.pallas_call(
        functools.partial(_inproj_kernel, tm=tm),
        grid=(m // tm, N_BIG // tn),
        in_specs=[
            pl.BlockSpec((tm, d), lambda i, j: (i, 0)),
            pl.BlockSpec((1, d), lambda i, j: (0, 0)),
            pl.BlockSpec((tn, d), lambda i, j: (j, 0)),
            pl.BlockSpec((LANES, d), lambda i, j: (0, 0)),
        ],
        out_specs=[
            pl.BlockSpec((tm, tn), lambda i, j: (i, j)),
            pl.BlockSpec((tm, LANES), lambda i, j: (i, 0)),
        ],
        out_shape=[
            jax.ShapeDtypeStruct((m, N_BIG), BF16),
            jax.ShapeDtypeStruct((m, LANES), F32),
        ],
        scratch_shapes=[pltpu.VMEM((tm, d), BF16)],
        compiler_params=pltpu.CompilerParams(
            dimension_semantics=("arbitrary", "arbitrary"),
            vmem_limit_bytes=VMEM_LIMIT),
        name="inproj",
    )(x2d, ln_w, w_big, w_small)


def _gates_kernel(ps_ref, alog_ref, dtb_ref, out_ref, out_t_ref, *, seq):
    c = GDN_CHUNK
    ri = lax.broadcasted_iota(jnp.int32, (c, c), 0)
    ci = lax.broadcasted_iota(jnp.int32, (c, c), 1)
    tri_l = jnp.where(ri >= ci, 1.0, 0.0).astype(BF16)
    tri_u = jnp.where(ri <= ci, 1.0, 0.0).astype(BF16)
    lane = lax.broadcasted_iota(jnp.int32, (c, LANES), 1)
    neg_a = -jnp.exp(alog_ref[...])

    for n in range(seq // c):
        a = ps_ref[n * c:(n + 1) * c, :]
        g = neg_a * _softplus(a + dtb_ref[...])
        beta = _sigmoid(a)
        parts = jnp.concatenate(_split3(g), axis=1)
        cl = _dot(tri_l, parts)
        cu = _dot(tri_u, parts)
        cl = cl[:, :LANES] + cl[:, LANES:2 * LANES] + cl[:, 2 * LANES:]
        cu = cu[:, :LANES] + cu[:, LANES:2 * LANES] + cu[:, 2 * LANES:]
        res = jnp.where(
            lane < LANE_GB, cl, jnp.where(lane < LANE_BF, cu, jnp.where(lane < LANE_RF, beta, a)))
        out_ref[n * c:(n + 1) * c, :] = jnp.concatenate(_split2(res), axis=1)
        out_t_ref[:, n * c:(n + 1) * c] = res.T


def _gates(ps, alog_vec, dtb_vec):
    b, t, _ = ps.shape
    return pl.pallas_call(
        functools.partial(_gates_kernel, seq=t),
        grid=(b,),
        in_specs=[
            pl.BlockSpec((None, t, LANES), lambda i: (i, 0, 0)),
            pl.BlockSpec((1, LANES), lambda i: (0, 0)),
            pl.BlockSpec((1, LANES), lambda i: (0, 0)),
        ],
        out_specs=[
            pl.BlockSpec((None, t, 2 * LANES), lambda i: (i, 0, 0)),
            pl.BlockSpec((None, LANES, t), lambda i: (i, 0, 0)),
        ],
        out_shape=[
            jax.ShapeDtypeStruct((b, t, 2 * LANES), BF16),
            jax.ShapeDtypeStruct((b, LANES, t), F32),
        ],
        compiler_params=pltpu.CompilerParams(
            dimension_semantics=("arbitrary",), vmem_limit_bytes=VMEM_LIMIT),
        name="gates",
    )(ps, alog_vec, dtb_vec)


def _block_diag(xp):
    c = xp.shape[0]
    z = jnp.zeros((c, c), xp.dtype)
    return jnp.concatenate([jnp.concatenate([xp[:, :c], z], axis=1),
                            jnp.concatenate([z, xp[:, c:]], axis=1)], axis=0)


def _odd_blocks(x, s):
    return jnp.concatenate([x[b * s:(b + 1) * s] for b in range(1, x.shape[0] // s, 2)], axis=0)


def _spread_odd(x_odd, s):
    z = jnp.zeros((s, x_odd.shape[1]), x_odd.dtype)
    parts = []
    for j in range(x_odd.shape[0] // s):
        parts += [z, x_odd[j * s:(j + 1) * s]]
    return jnp.concatenate(parts, axis=0)


def _sub_from_odd(x, r_odd, s):
    parts = []
    for b in range(x.shape[0] // s):
        blk = x[b * s:(b + 1) * s]
        if b % 2:
            blk = blk - r_odd[(b // 2) * s:(b // 2 + 1) * s]
        parts.append(blk)
    return jnp.concatenate(parts, axis=0)


GDN_FULL_LEVELS = 3
GDN_BF16_ROWS = 16


def _unit_tri_inverse_stages(lps, masks_ref, mbf_ref, modd_ref, moddbf_ref, eye_p, out):
    ts = [eye_p - lp * masks_ref[...] for lp in lps]
    lbs = [lp.astype(BF16) for lp in lps]
    for lvl in range(1, 7):
        s = 1 << lvl
        tbs = [t.astype(BF16) for t in ts]
        if lvl < GDN_FULL_LEVELS:
            ms = [_dot(lb * mbf_ref[lvl - 1], _block_diag(tb)) for lb, tb in zip(lbs, tbs)]
            yield
            ts = [t - _dot(tb, _block_diag(m.astype(BF16))) for t, tb, m in zip(ts, tbs, ms)]
        else:
            if s < GDN_BF16_ROWS:
                cs = [(_odd_blocks(lp, s) * modd_ref[...]).astype(BF16) for lp in lps]
            else:
                mo = moddbf_ref[lvl - GDN_FULL_LEVELS - 1]
                cs = [_odd_blocks(lb, s) * mo for lb in lbs]
            ms = [_dot(cm, _block_diag(tb)) for cm, tb in zip(cs, tbs)]
            yield
            ts = [_sub_from_odd(t, _dot(_odd_blocks(t, s).astype(BF16),
                                        _block_diag(_spread_odd(m, s).astype(BF16))), s)
                  for t, m in zip(ts, ms)]
        yield
    out.extend(ts)


def _gdn_kernel(q_ref, k_ref, v_ref, z_ref, gch_ref, gct_ref, cwq_ref, cwk_ref, cwv_ref, nw_ref,
                o_ref, xpq, xpk, xpv, qs, ks, vs, kks, qks, aqs, bos, egls, osum, masks, mbf, modd,
                moddbf, selc, rowbuf, *, seq, unroll, stagger):
    c = GDN_CHUNK
    nc = seq // c
    h = pl.program_id(1)
    ri = lax.broadcasted_iota(jnp.int32, (c, c), 0)
    ci = lax.broadcasted_iota(jnp.int32, (c, c), 1)
    lower, upper = ri >= ci, ri <= ci
    rp = lax.broadcasted_iota(jnp.int32, (c, 2 * c), 0)
    cp = lax.broadcasted_iota(jnp.int32, (c, 2 * c), 1) & (c - 1)
    eye_p = jnp.where(rp == cp, 1.0, 0.0)

    for lvl in range(7):
        m = jnp.where(((rp >> (lvl + 1)) == (cp >> (lvl + 1))) & ((rp >> lvl) != (cp >> lvl)), 1.0, 0.0)
        if lvl == 0:
            masks[...] = m
        elif lvl < GDN_FULL_LEVELS:
            mbf[lvl - 1] = m.astype(BF16)
        elif (1 << lvl) < GDN_BF16_ROWS:
            modd[...] = _odd_blocks(m, 1 << lvl)
        else:
            moddbf[lvl - GDN_FULL_LEVELS - 1] = _odd_blocks(m, 1 << lvl).astype(BF16)

    l2 = lax.broadcasted_iota(jnp.int32, (2 * c, 2 * c), 0) & (c - 1)
    cb = lax.broadcasted_iota(jnp.int32, (2 * c, 2 * c), 1) >> 7
    lanes_g = (h + LANE_GF, h + LANE_GB)
    lanes_b = (h + LANE_BF, h + LANE_BB)
    for d in range(2):
        selc[d] = jnp.where(l2 == jnp.where(cb == 0, lanes_g[d], lanes_b[d]), 1.0, 0.0).astype(BF16)

    sub = lax.broadcasted_iota(jnp.int32, (8, seq), 0)
    for idx, lane0 in enumerate((LANE_GF, LANE_GB, LANE_BB)):
        rowbuf[idx:idx + 1, :] = jnp.sum(
            jnp.where(sub == h, gct_ref[lane0:lane0 + 8, :], 0.0), axis=0, keepdims=True)

    zeros8 = jnp.zeros((8, LANES), F32)
    rb = 2 * c
    nrb = seq // rb

    def conv_fill(x_ref, xp):
        xp[pl.ds(0, 8), :] = zeros8
        xp[pl.ds(seq + 8, 8), :] = zeros8
        xp[pl.ds(8, seq), :] = x_ref[...].astype(F32)

    def conv_silu(xp, cw_ref, r):
        base = 8 - CONV_K // 2 + r * rb
        acc = cw_ref[0:1, :] * xp[pl.ds(base, rb), :]
        for j in range(1, CONV_K):
            acc = acc + cw_ref[j:j + 1, :] * xp[pl.ds(base + j, rb), :]
        return _silu(acc)

    def l2norm(y):
        return y * lax.rsqrt(jnp.sum(y * y, axis=-1, keepdims=True) + NORM_EPS)

    conv_fill(q_ref, xpq)
    conv_fill(k_ref, xpk)
    conv_fill(v_ref, xpv)
    for r in range(nrb):
        qs[r * rb:(r + 1) * rb, :] = l2norm(conv_silu(xpq, cwq_ref, r)) * (GDN_HEAD_DIM ** -0.5)
        ks[r * rb:(r + 1) * rb, :] = l2norm(conv_silu(xpk, cwk_ref, r))
        for n in range(r * rb // c, (r + 1) * rb // c):
            kcb = ks[n * c:(n + 1) * c, :].astype(BF16)
            kks[n] = _dot_nt(kcb, kcb)
            qks[n] = _dot_nt(qs[n * c:(n + 1) * c, :].astype(BF16), kcb)
    osum[...] = jnp.zeros((seq, LANES), F32)
    v_pending = list(range(nrb))

    def conv_v_block():
        r = v_pending.pop(0)
        vs[r * rb:(r + 1) * rb, :] = conv_silu(xpv, cwv_ref, r)

    def row_form(idx, n):
        return jnp.broadcast_to(rowbuf[idx:idx + 1, n * c:(n + 1) * c], (c, c))

    def prep(trip):
        pairs = [(trip * unroll + u, nc - 1 - trip * unroll - u) for u in range(unroll)]
        work, lps = [], []
        for pair in pairs:
            item = []
            for d, n in enumerate(pair):
                sel = _dot(gch_ref[n * c:(n + 1) * c, :], selc[d])
                item.append([n, sel[:, :c], sel[:, c:]])
            fw, bw = item
            n, gcol, beta = fw
            decay = jnp.exp(jnp.where(lower, gcol - row_form(0, n), MASKED_LOG))
            l_f = kks[n] * decay * beta
            fw.append(qks[n] * decay)
            n, gcol, beta = bw
            grow = row_form(1, n)
            decay = jnp.exp(jnp.where(upper, gcol - grow, MASKED_LOG))
            l_b = kks[n] * jnp.exp(jnp.where(lower, grow - gcol, MASKED_LOG)) * row_form(2, n)
            bw.append(qks[n] * decay)
            lps.append(jnp.concatenate([l_f, l_b], axis=1))
            work.append(item)
            yield
        tps = []
        yield from _unit_tri_inverse_stages(lps, masks, mbf, modd, moddbf, eye_p, tps)
        assert not v_pending
        sols = []
        for item, tp in zip(work, tps):
            for d, (n, gcol, beta, attn) in enumerate(item):
                t_inv = tp[:, :c] if d == 0 else tp[:, c:].T
                kc = ks[n * c:(n + 1) * c, :]
                eg = jnp.exp(gcol)
                rhs = jnp.concatenate([(vs[n * c:(n + 1) * c, :] * beta).astype(BF16),
                                       (kc * beta * eg).astype(BF16)], axis=1)
                sols.append(_dot(t_inv.astype(BF16), rhs))
        yield
        k = 0
        for item in work:
            for d, (n, gcol, beta, attn) in enumerate(item):
                kc = ks[n * c:(n + 1) * c, :]
                glast = gcol[c - 1:c, :] if d == 0 else gcol[0:1, :]
                kdt = (kc * jnp.exp(glast - gcol)).T
                lhs = jnp.concatenate([kdt.astype(BF16), attn.astype(BF16)], axis=0)
                z = _dot(lhs, sols[k].astype(BF16))
                k += 1
                qd = qs[n * c:(n + 1) * c, :] * jnp.exp(gcol)
                aqs[d, n, 0:c, :] = (-z[:c, c:]).astype(BF16)
                aqs[d, n, c:2 * c, :] = (qd - z[c:, c:]).astype(BF16)
                bos[d, n] = z[:, :c]
                egls[d, n * 8:(n + 1) * 8, :] = jnp.broadcast_to(jnp.exp(glast), (8, LANES))
        yield

    states = [jnp.zeros((GDN_HEAD_DIM, GDN_HEAD_DIM), F32)] * 2

    def scan_step(trip, u):
        for d in range(2):
            n = trip * unroll + u if d == 0 else nc - 1 - trip * unroll - u
            r = _dot(aqs[d, n], states[d].astype(BF16)) + bos[d, n]
            states[d] = states[d] * egls[d, n * 8:n * 8 + 1, :] + r[:c]
            osum[n * c:(n + 1) * c, :] += r[c:]

    trips = nc // unroll
    active, scan_queue, started, tick = {}, [], 0, 0
    while started < trips or active or scan_queue:
        if started < trips and tick >= started * stagger:
            active[started] = prep(started)
            started += 1
        for trip in sorted(active):
            if next(active[trip], "done") == "done":
                del active[trip]
                scan_queue.extend((trip, u) for u in range(unroll))
        if v_pending:
            conv_v_block()
        if scan_queue:
            scan_step(*scan_queue.pop(0))
        tick += 1

    o = osum[...]
    y = o * lax.rsqrt(jnp.mean(o * o, axis=-1, keepdims=True) + NORM_EPS) * nw_ref[...]
    o_ref[...] = (y * _silu(z_ref[...].astype(F32))).astype(BF16)


def _gdn(pbig, gch, gct, conv_w, norm_w, *, unroll=4, stagger=5):
    b, t, _ = pbig.shape
    c = GDN_CHUNK
    nc = t // c
    hd = GDN_HEAD_DIM
    nh = GDN_HEADS
    assert nc % unroll == 0
    return pl.pallas_call(
        functools.partial(_gdn_kernel, seq=t, unroll=unroll, stagger=stagger),
        grid=(b, nh),
        in_specs=[
            pl.BlockSpec((None, t, hd), lambda i, j: (i, 0, j)),
            pl.BlockSpec((None, t, hd), lambda i, j: (i, 0, nh + j)),
            pl.BlockSpec((None, t, hd), lambda i, j: (i, 0, 2 * nh + j)),
            pl.BlockSpec((None, t, hd), lambda i, j: (i, 0, COL_Z // hd + j)),
            pl.BlockSpec((None, t, 2 * LANES), lambda i, j: (i, 0, 0)),
            pl.BlockSpec((None, LANES, t), lambda i, j: (i, 0, 0)),
            pl.BlockSpec((CONV_K, hd), lambda i, j: (0, j)),
            pl.BlockSpec((CONV_K, hd), lambda i, j: (0, nh + j)),
            pl.BlockSpec((CONV_K, hd), lambda i, j: (0, 2 * nh + j)),
            pl.BlockSpec((1, hd), lambda i, j: (0, 0)),
        ],
        out_specs=pl.BlockSpec((None, t, hd), lambda i, j: (i, 0, j)),
        out_shape=jax.ShapeDtypeStruct((b, t, GDN_WIDTH), BF16),
        scratch_shapes=[
            pltpu.VMEM((t + 16, hd), F32),
            pltpu.VMEM((t + 16, hd), F32),
            pltpu.VMEM((t + 16, hd), F32),
            pltpu.VMEM((t, hd), F32),
            pltpu.VMEM((t, hd), F32),
            pltpu.VMEM((t, hd), F32),
            pltpu.VMEM((nc, c, c), F32),
            pltpu.VMEM((nc, c, c), F32),
            pltpu.VMEM((2, nc, 2 * c, hd), BF16),
            pltpu.VMEM((2, nc, 2 * c, hd), F32),
            pltpu.VMEM((2, nc * 8, LANES), F32),
            pltpu.VMEM((t, hd), F32),
            pltpu.VMEM((c, 2 * c), F32),
            pltpu.VMEM((GDN_FULL_LEVELS - 1, c, 2 * c), BF16),
            pltpu.VMEM((c // 2, 2 * c), F32),
            pltpu.VMEM((3, c // 2, 2 * c), BF16),
            pltpu.VMEM((2, 2 * c, 2 * c), BF16),
            pltpu.VMEM((8, t), F32),
        ],
        compiler_params=pltpu.CompilerParams(
            dimension_semantics=("arbitrary", "arbitrary"), vmem_limit_bytes=VMEM_LIMIT),
        name="gdn",
    )(pbig, pbig, pbig, pbig, gch, gct, conv_w, conv_w, conv_w, norm_w)


def _gla_kernel(q_ref, k_ref, v_ref, g_ref, gc_ref, w2f_ref, w2b_ref, b2f_ref, b2b_ref, nw_ref,
                o_ref, tris, sst, osum, *, seq, unroll):
    c2 = 2 * GLA_CHUNK
    npair = seq // c2
    dk = GLA_HEAD_K
    dv = GLA_HEAD_V
    ri = lax.broadcasted_iota(jnp.int32, (c2, c2), 0)
    ci = lax.broadcasted_iota(jnp.int32, (c2, c2), 1)
    same = (ri >> 6) == (ci >> 6)
    incls = [same & (ri >= ci), same & (ri <= ci)]
    crosses = [(ri >= GLA_CHUNK) & (ci < GLA_CHUNK), (ri < GLA_CHUNK) & (ci >= GLA_CHUNK)]
    row_k = lax.broadcasted_iota(jnp.int32, (c2, dk), 0)
    scanned_first = [row_k < GLA_CHUNK, row_k >= GLA_CHUNK]
    total_rows = [(GLA_CHUNK - 1, c2 - 1), (GLA_CHUNK, 0)]
    w2_refs = (w2f_ref, w2b_ref)
    b2_refs = (b2f_ref, b2b_ref)

    for d in range(2):
        tris[d] = jnp.where(incls[d], 1.0, 0.0).astype(BF16)
    osum[...] = jnp.zeros((seq, dv), F32)
    sst[...] = jnp.zeros((2, dk, dv), F32)

    def pairs(it, _):
        items = []
        for u in range(unroll):
            for d in range(2):
                pp = it * unroll + u
                pp = pp if d == 0 else npair - 1 - pp
                rows = pl.ds(pl.multiple_of(pp * c2, c2), c2)
                pre = _dot(gc_ref[rows, :], w2_refs[d][...]) + b2_refs[d][...]
                items.append([d, rows, pre])
        for x in items:
            gk = _log_sigmoid(x[2]) / GLA_GATE_NORMALIZER
            cs = _dot(tris[x[0]], jnp.concatenate(_split3(gk), axis=1))
            x[2] = cs[:, :dk] + cs[:, dk:2 * dk] + cs[:, 2 * dk:]
        for x in items:
            d, rows, gcum = x
            r1, r2 = total_rows[d]
            tot1, tot2 = gcum[r1:r1 + 1, :], gcum[r2:r2 + 1, :]
            q = q_ref[rows, :].astype(F32) * (dk ** -0.5)
            k = k_ref[rows, :].astype(F32)
            qg = q * jnp.exp(gcum)
            qgb = qg.astype(BF16)
            kg = (k * jnp.exp(-gcum)).astype(BF16)
            kd = k * jnp.exp(jnp.where(scanned_first[d], tot1, tot2) - gcum)
            attn = jnp.where(incls[d], _dot_nt(qgb, kg),
                             jnp.where(crosses[d], _dot_nt(qgb, kd.astype(BF16)), 0.0))
            q_in = jnp.where(scanned_first[d], qg, qg * jnp.exp(tot1)).astype(BF16)
            kd_out = jnp.where(scanned_first[d], kd * jnp.exp(tot2), kd)
            gt = gcum.T
            e_col = jnp.exp(gt[:, r1:r1 + 1] + gt[:, r2:r2 + 1])
            x[2:] = [q_in, attn, kd_out.T.astype(BF16), e_col]
        for x in items:
            d, rows, q_in, attn, kdt, e_col = x
            vb = v_ref[rows, :]
            x[3] = _dot(attn.astype(BF16), vb)
            x[4] = _dot(kdt, vb)
        states = [sst[0], sst[1]]
        for x in items:
            d, rows, q_in, intra, upd, e_col = x
            osum[rows, :] += _dot(q_in, states[d].astype(BF16)) + intra
            states[d] = states[d] * e_col + upd
        sst[0] = states[0]
        sst[1] = states[1]
        return 0

    lax.fori_loop(0, npair // unroll, pairs, 0)

    o = osum[...]
    y = o * lax.rsqrt(jnp.mean(o * o, axis=-1, keepdims=True) + NORM_EPS) * nw_ref[...]
    o_ref[...] = (y * _silu(g_ref[...].astype(F32))).astype(BF16)


def _gla(pbig, gc, w2f, w2b, b2f, b2b, norm_w, *, unroll=4):
    b, t, _ = pbig.shape
    dk = GLA_HEAD_K
    dv = GLA_HEAD_V
    return pl.pallas_call(
        functools.partial(_gla_kernel, seq=t, unroll=unroll),
        grid=(b, GLA_HEADS),
        in_specs=[
            pl.BlockSpec((None, t, dk), lambda i, j: (i, 0, COL_QB // dk + j)),
            pl.BlockSpec((None, t, dk), lambda i, j: (i, 0, COL_KB // dk + j)),
            pl.BlockSpec((None, t, dv), lambda i, j: (i, 0, COL_VB // dv + j)),
            pl.BlockSpec((None, t, dv), lambda i, j: (i, 0, COL_GB // dv + j)),
            pl.BlockSpec((None, t, LANES), lambda i, j: (i, 0, 0)),
            pl.BlockSpec((LANES, dk), lambda i, j: (0, j)),
            pl.BlockSpec((LANES, dk), lambda i, j: (0, j)),
            pl.BlockSpec((1, dk), lambda i, j: (0, j)),
            pl.BlockSpec((1, dk), lambda i, j: (0, j)),
            pl.BlockSpec((1, dv), lambda i, j: (0, 0)),
        ],
        out_specs=pl.BlockSpec((None, t, dv), lambda i, j: (i, 0, j)),
        out_shape=jax.ShapeDtypeStruct((b, t, GLA_VAL_DIM), BF16),
        scratch_shapes=[
            pltpu.VMEM((2, 2 * GLA_CHUNK, 2 * GLA_CHUNK), BF16),
            pltpu.VMEM((2, dk, dv), F32),
            pltpu.VMEM((t, dv), F32),
        ],
        compiler_params=pltpu.CompilerParams(
            dimension_semantics=("arbitrary", "arbitrary"), vmem_limit_bytes=VMEM_LIMIT),
        name="gla",
    )(pbig, pbig, pbig, pbig, gc, w2f, w2b, b2f, b2b, norm_w)


def _merge_kernel(oa_ref, ob_ref, ga_ref, gb_ref, x_ref, wa_ref, wb_ref, wo_ref, lnw_ref, out_ref):
    ya = _dot(oa_ref[...], wa_ref[...])
    yb = _dot(ob_ref[...], wb_ref[...])
    merged = (_sigmoid(ga_ref[...].astype(F32)) * ya + _sigmoid(gb_ref[...].astype(F32)) * yb)
    out = _dot(merged.astype(BF16), wo_ref[...])
    y = out * lax.rsqrt(jnp.mean(out * out, axis=-1, keepdims=True) + NORM_EPS) * lnw_ref[...]
    out_ref[...] = x_ref[...] + y


def _merge(oa, ob, pbig2d, x2d, wa, wb, wo, ln_w, *, tm=512):
    m, d = x2d.shape
    return pl.pallas_call(
        _merge_kernel,
        grid=(m // tm,),
        in_specs=[
            pl.BlockSpec((tm, d), lambda i: (i, 0)),
            pl.BlockSpec((tm, d), lambda i: (i, 0)),
            pl.BlockSpec((tm, d), lambda i: (i, COL_GATE_A // D_MODEL)),
            pl.BlockSpec((tm, d), lambda i: (i, COL_GATE_B // D_MODEL)),
            pl.BlockSpec((tm, d), lambda i: (i, 0)),
            pl.BlockSpec((d, d), lambda i: (0, 0)),
            pl.BlockSpec((d, d), lambda i: (0, 0)),
            pl.BlockSpec((d, d), lambda i: (0, 0)),
            pl.BlockSpec((1, d), lambda i: (0, 0)),
        ],
        out_specs=pl.BlockSpec((tm, d), lambda i: (i, 0)),
        out_shape=jax.ShapeDtypeStruct((m, d), F32),
        compiler_params=pltpu.CompilerParams(
            dimension_semantics=("arbitrary",), vmem_limit_bytes=VMEM_LIMIT),
        name="merge",
    )(oa, ob, pbig2d, pbig2d, x2d, wa, wb, wo, ln_w)


def _layer(x, ln_pre_w, w_in, conv_w, a_log_fwd, a_log_bwd, dt_bias_fwd, dt_bias_bwd,
           gdn_norm_w, w_proj_gdn, gk_w2_fwd, gk_b2_fwd, gk_w2_bwd, gk_b2_bwd,
           gla_norm_w, w_proj_gla, w_out, ln_post_w):
    b, t, d = x.shape
    m = b * t
    o = _OFF
    wt = w_in.T
    w_big = _wprep(wt)
    w_small = jnp.concatenate(
        [wt[o[2]:o[6]], wt[o[10]:o[12]],
         jnp.zeros((LANES - 4 * GDN_HEADS - 2 * GLA_GATE_RANK, d), w_in.dtype)], axis=0).astype(BF16)
    pad = jnp.zeros((LANES - 2 * GDN_HEADS,), F32)
    alog_vec = jnp.concatenate([a_log_fwd, a_log_bwd, pad]).reshape(1, LANES)
    dtb_vec = jnp.concatenate([dt_bias_fwd, dt_bias_bwd, pad]).reshape(1, LANES)

    def w2_ext(w2, lane0):
        z = jnp.zeros((LANES, GLA_KEY_DIM), F32)
        return z.at[lane0:lane0 + GLA_GATE_RANK, :].set(w2).astype(BF16)

    x2d = x.reshape(m, d)
    pbig, psmall = _inproj(x2d, ln_pre_w.reshape(1, d), w_big, w_small)
    gch, gct = _gates(psmall.reshape(b, t, LANES), alog_vec, dtb_vec)
    pbig3 = pbig.reshape(b, t, N_BIG)
    oa = _gdn(pbig3, gch, gct, conv_w, gdn_norm_w.reshape(1, GDN_HEAD_DIM))
    ob = _gla(pbig3, gch, w2_ext(gk_w2_fwd, LANE_RF), w2_ext(gk_w2_bwd, LANE_RB),
              gk_b2_fwd.reshape(1, GLA_KEY_DIM), gk_b2_bwd.reshape(1, GLA_KEY_DIM),
              gla_norm_w.reshape(1, GLA_HEAD_V))
    out = _merge(oa.reshape(m, GDN_WIDTH), ob.reshape(m, GLA_VAL_DIM), pbig, x2d,
                 w_proj_gdn.astype(BF16), w_proj_gla.astype(BF16), w_out.astype(BF16),
                 ln_post_w.reshape(1, d))
    return out.reshape(b, t, d)


def kernel(x, ln_pre_w, w_in, conv_w, a_log_fwd, a_log_bwd, dt_bias_fwd, dt_bias_bwd, gdn_norm_w,
           w_proj_gdn, gk_w2_fwd, gk_b2_fwd, gk_w2_bwd, gk_b2_bwd, gla_norm_w, w_proj_gla, w_out,
           ln_post_w):
    h = x
    for l in range(ln_pre_w.shape[0]):
        h = _layer(h, ln_pre_w[l], w_in[l], conv_w[l], a_log_fwd[l], a_log_bwd[l],
                   dt_bias_fwd[l], dt_bias_bwd[l], gdn_norm_w[l], w_proj_gdn[l],
                   gk_w2_fwd[l], gk_b2_fwd[l], gk_w2_bwd[l], gk_b2_bwd[l],
                   gla_norm_w[l], w_proj_gla[l], w_out[l], ln_post_w[l])
    return h
```

```python
import functools

import jax
import jax.numpy as jnp
from jax import lax
from jax.experimental import pallas as pl
from jax.experimental.pallas import tpu as pltpu

F32 = jnp.float32
BF16 = jnp.bfloat16

D_MODEL = 1024
GDN_HEADS = 8
GDN_HEAD_DIM = 128
GDN_WIDTH = GDN_HEADS * GDN_HEAD_DIM
CONV_K = 5
GLA_HEADS = 4
GLA_KEY_DIM = D_MODEL // 2
GLA_VAL_DIM = D_MODEL
GLA_HEAD_K = GLA_KEY_DIM // GLA_HEADS
GLA_HEAD_V = GLA_VAL_DIM // GLA_HEADS
GLA_GATE_RANK = 16
GLA_GATE_NORMALIZER = 16.0
GLA_CHUNK = 64
NORM_EPS = 1e-6
MASKED_LOG = -1e30

LANES = 128
GDN_CHUNK = LANES
VMEM_LIMIT = 56 * 1024 * 1024

_IN_SIZES = [3 * GDN_WIDTH, GDN_WIDTH, GDN_HEADS, GDN_HEADS, GDN_HEADS, GDN_HEADS,
             GLA_KEY_DIM, GLA_KEY_DIM, GLA_VAL_DIM, GLA_VAL_DIM,
             GLA_GATE_RANK, GLA_GATE_RANK, D_MODEL, D_MODEL]
_OFF = [0]
for _s in _IN_SIZES:
    _OFF.append(_OFF[-1] + _s)
N_BIG = 3 * GDN_WIDTH + GDN_WIDTH + 2 * GLA_KEY_DIM + 2 * GLA_VAL_DIM + 2 * D_MODEL
COL_Z = 3 * GDN_WIDTH
COL_QB = COL_Z + GDN_WIDTH
COL_KB = COL_QB + GLA_KEY_DIM
COL_VB = COL_KB + GLA_KEY_DIM
COL_GB = COL_VB + GLA_VAL_DIM
COL_GATE_A = COL_GB + GLA_VAL_DIM
COL_GATE_B = COL_GATE_A + D_MODEL
LANE_GF, LANE_GB, LANE_BF, LANE_BB = 0, 8, 16, 24
LANE_RF, LANE_RB = 32, 48


def _dot(a, b):
    return jnp.dot(a, b, preferred_element_type=F32)


def _dot_nt(a, b):
    return lax.dot_general(a, b, (((1,), (1,)), ((), ())), preferred_element_type=F32)


def _split2(x):
    hi = x.astype(BF16)
    lo = (x - hi.astype(F32)).astype(BF16)
    return hi, lo


def _split3(x):
    hi = x.astype(BF16)
    r = x - hi.astype(F32)
    mid = r.astype(BF16)
    lo = (r - mid.astype(F32)).astype(BF16)
    return hi, mid, lo


def _sigmoid(x):
    return jax.nn.sigmoid(x)


def _silu(x):
    return x * _sigmoid(x)


def _softplus(x):
    return jnp.maximum(x, 0.0) + jnp.log(1.0 + jnp.exp(-jnp.abs(x)))


def _log_sigmoid(x):
    return jnp.minimum(x, 0.0) - jnp.log(1.0 + jnp.exp(-jnp.abs(x)))


def _wide_tile_row(j, tn):
    unit = _OFF[6] - _OFF[2]
    assert _OFF[12] - _OFF[10] == unit and _OFF[2] % unit == 0 and _OFF[10] % unit == 0 and tn % unit == 0
    start = j * (tn // unit)
    for first_wide_col in (_OFF[2], _OFF[10] - unit):
        start = start + jnp.where(j * tn >= first_wide_col, 1, 0)
    return start * unit


def _wprep_kernel(w_ref, na_ref, nb_ref, o_ref, osmall_ref):
    o_ref[...] = w_ref[...].astype(BF16)

    @pl.when(pl.program_id(0) == 0)
    def _():
        unit = na_ref.shape[0]
        osmall_ref[0:unit, :] = na_ref[...].astype(BF16)
        osmall_ref[unit:2 * unit, :] = nb_ref[...].astype(BF16)
        osmall_ref[2 * unit:, :] = jnp.zeros((LANES - 2 * unit, osmall_ref.shape[1]), BF16)


def _wprep(w_t, *, tn=1024):
    n, d = w_t.shape
    unit = _OFF[6] - _OFF[2]
    assert all(seg % tn == 0 for seg in (_OFF[2], _OFF[10] - _OFF[6], _OFF[14] - _OFF[12]))
    return pl.pallas_call(
        _wprep_kernel,
        grid=(N_BIG // tn,),
        in_specs=[
            pl.BlockSpec((pl.Element(tn), pl.Element(d)), lambda j: (_wide_tile_row(j, tn), 0)),
            pl.BlockSpec((unit, d), lambda j: (_OFF[2] // unit, 0)),
            pl.BlockSpec((unit, d), lambda j: (_OFF[10] // unit, 0)),
        ],
        out_specs=[
            pl.BlockSpec((tn, d), lambda j: (j, 0)),
            pl.BlockSpec((LANES, d), lambda j: (0, 0)),
        ],
        out_shape=[
            jax.ShapeDtypeStruct((N_BIG, d), BF16),
            jax.ShapeDtypeStruct((LANES, d), BF16),
        ],
        compiler_params=pltpu.CompilerParams(
            dimension_semantics=("arbitrary",), vmem_limit_bytes=VMEM_LIMIT),
        name="wprep",
    )(w_t, w_t, w_t)


def _inproj_kernel(x_ref, lnw_ref, wbig_ref, wsmall_ref, pbig_ref, psmall_ref, h_ref, *, tm):
    j = pl.program_id(1)

    @pl.when(j == 0)
    def _():
        def body(r, _):
            rows = pl.ds(pl.multiple_of(r * LANES, LANES), LANES)
            x = x_ref[rows, :]
            ms = jnp.mean(x * x, axis=-1, keepdims=True)
            h_ref[rows, :] = (x * lax.rsqrt(ms + NORM_EPS) * lnw_ref[...]).astype(BF16)
            return 0
        lax.fori_loop(0, tm // LANES, body, 0)
        psmall_ref[...] = _dot_nt(h_ref[...], wsmall_ref[...])

    pbig_ref[...] = _dot_nt(h_ref[...], wbig_ref[...]).astype(BF16)


def _inproj(x2d, ln_w, w_big, w_small, *, tm=1024, tn=3072):
    m, d = x2d.shape
    return pl.pallas_call(
        functools.partial(_inproj_kernel, tm=tm),
        grid=(m // tm, N_BIG // tn),
        in_specs=[
            pl.BlockSpec((tm, d), lambda i, j: (i, 0)),
            pl.BlockSpec((1, d), lambda i, j: (0, 0)),
            pl.BlockSpec((tn, d), lambda i, j: (j, 0)),
            pl.BlockSpec((LANES, d), lambda i, j: (0, 0)),
        ],
        out_specs=[
            pl.BlockSpec((tm, tn), lambda i, j: (i, j)),
            pl.BlockSpec((tm, LANES), lambda i, j: (i, 0)),
        ],
        out_shape=[
            jax.ShapeDtypeStruct((m, N_BIG), BF16),
            jax.ShapeDtypeStruct((m, LANES), F32),
        ],
        scratch_shapes=[pltpu.VMEM((tm, d), BF16)],
        compiler_params=pltpu.CompilerParams(
            dimension_semantics=("arbitrary", "arbitrary"),
            vmem_limit_bytes=VMEM_LIMIT),
        name="inproj",
    )(x2d, ln_w, w_big, w_small)


def _gates_kernel(ps_ref, alog_ref, dtb_ref, out_ref, out_t_ref, *, seq):
    c = GDN_CHUNK
    ri = lax.broadcasted_iota(jnp.int32, (c, c), 0)
    ci = lax.broadcasted_iota(jnp.int32, (c, c), 1)
    tri_l = jnp.where(ri >= ci, 1.0, 0.0).astype(BF16)
    tri_u = jnp.where(ri <= ci, 1.0, 0.0).astype(BF16)
    lane = lax.broadcasted_iota(jnp.int32, (c, LANES), 1)
    neg_a = -jnp.exp(alog_ref[...])

    for n in range(seq // c):
        a = ps_ref[n * c:(n + 1) * c, :]
        g = neg_a * _softplus(a + dtb_ref[...])
        beta = _sigmoid(a)
        parts = jnp.concatenate(_split3(g), axis=1)
        cl = _dot(tri_l, parts)
        cu = _dot(tri_u, parts)
        cl = cl[:, :LANES] + cl[:, LANES:2 * LANES] + cl[:, 2 * LANES:]
        cu = cu[:, :LANES] + cu[:, LANES:2 * LANES] + cu[:, 2 * LANES:]
        res = jnp.where(
            lane < LANE_GB, cl, jnp.where(lane < LANE_BF, cu, jnp.where(lane < LANE_RF, beta, a)))
        out_ref[n * c:(n + 1) * c, :] = jnp.concatenate(_split2(res), axis=1)
        out_t_ref[:, n * c:(n + 1) * c] = res.T


def _gates(ps, alog_vec, dtb_vec):
    b, t, _ = ps.shape
    return pl.pallas_call(
        functools.partial(_gates_kernel, seq=t),
        grid=(b,),
        in_specs=[
            pl.BlockSpec((None, t, LANES), lambda i: (i, 0, 0)),
            pl.BlockSpec((1, LANES), lambda i: (0, 0)),
            pl.BlockSpec((1, LANES), lambda i: (0, 0)),
        ],
        out_specs=[
            pl.BlockSpec((None, t, 2 * LANES), lambda i: (i, 0, 0)),
            pl.BlockSpec((None, LANES, t), lambda i: (i, 0, 0)),
        ],
        out_shape=[
            jax.ShapeDtypeStruct((b, t, 2 * LANES), BF16),
            jax.ShapeDtypeStruct((b, LANES, t), F32),
        ],
        compiler_params=pltpu.CompilerParams(
            dimension_semantics=("arbitrary",), vmem_limit_bytes=VMEM_LIMIT),
        name="gates",
    )(ps, alog_vec, dtb_vec)


def _block_diag(xp):
    c = xp.shape[0]
    z = jnp.zeros((c, c), xp.dtype)
    return jnp.concatenate([jnp.concatenate([xp[:, :c], z], axis=1),
                            jnp.concatenate([z, xp[:, c:]], axis=1)], axis=0)


def _odd_blocks(x, s):
    return jnp.concatenate([x[b * s:(b + 1) * s] for b in range(1, x.shape[0] // s, 2)], axis=0)


def _spread_odd(x_odd, s):
    z = jnp.zeros((s, x_odd.shape[1]), x_odd.dtype)
    parts = []
    for j in range(x_odd.shape[0] // s):
        parts += [z, x_odd[j * s:(j + 1) * s]]
    return jnp.concatenate(parts, axis=0)


def _sub_from_odd(x, r_odd, s):
    parts = []
    for b in range(x.shape[0] // s):
        blk = x[b * s:(b + 1) * s]
        if b % 2:
            blk = blk - r_odd[(b // 2) * s:(b // 2 + 1) * s]
        parts.append(blk)
    return jnp.concatenate(parts, axis=0)


GDN_FULL_LEVELS = 3
GDN_BF16_ROWS = 16


def _unit_tri_inverse_stages(lps, masks_ref, mbf_ref, modd_ref, moddbf_ref, eye_p, out):
    ts = [eye_p - lp * masks_ref[...] for lp in lps]
    lbs = [lp.astype(BF16) for lp in lps]
    for lvl in range(1, 7):
        s = 1 << lvl
        tbs = [t.astype(BF16) for t in ts]
        if lvl < GDN_FULL_LEVELS:
            ms = [_dot(lb * mbf_ref[lvl - 1], _block_diag(tb)) for lb, tb in zip(lbs, tbs)]
            yield
            ts = [t - _dot(tb, _block_diag(m.astype(BF16))) for t, tb, m in zip(ts, tbs, ms)]
        else:
            if s < GDN_BF16_ROWS:
                cs = [(_odd_blocks(lp, s) * modd_ref[...]).astype(BF16) for lp in lps]
            else:
                mo = moddbf_ref[lvl - GDN_FULL_LEVELS - 1]
                cs = [_odd_blocks(lb, s) * mo for lb in lbs]
            ms = [_dot(cm, _block_diag(tb)) for cm, tb in zip(cs, tbs)]
            yield
            ts = [_sub_from_odd(t, _dot(_odd_blocks(t, s).astype(BF16),
                                        _block_diag(_spread_odd(m, s).astype(BF16))), s)
                  for t, m in zip(ts, ms)]
        yield
    out.extend(ts)


def _gdn_kernel(q_ref, k_ref, v_ref, z_ref, gch_ref, gct_ref, cwq_ref, cwk_ref, cwv_ref, nw_ref,
                o_ref, xpq, xpk, xpv, qs, ks, vs, kks, qks, aqs, bos, egls, osum, masks, mbf, modd,
                moddbf, selc, rowbuf, *, seq, unroll, stagger, heads, head_offset):
    c = GDN_CHUNK
    nc = seq // c
    ri = lax.broadcasted_iota(jnp.int32, (c, c), 0)
    ci = lax.broadcasted_iota(jnp.int32, (c, c), 1)
    lower, upper = ri >= ci, ri <= ci
    rp = lax.broadcasted_iota(jnp.int32, (c, 2 * c), 0)
    cp = lax.broadcasted_iota(jnp.int32, (c, 2 * c), 1) & (c - 1)
    eye_p = jnp.where(rp == cp, 1.0, 0.0)

    for lvl in range(7):
        m = jnp.where(((rp >> (lvl + 1)) == (cp >> (lvl + 1))) & ((rp >> lvl) != (cp >> lvl)), 1.0, 0.0)
        if lvl == 0:
            masks[...] = m
        elif lvl < GDN_FULL_LEVELS:
            mbf[lvl - 1] = m.astype(BF16)
        elif (1 << lvl) < GDN_BF16_ROWS:
            modd[...] = _odd_blocks(m, 1 << lvl)
        else:
            moddbf[lvl - GDN_FULL_LEVELS - 1] = _odd_blocks(m, 1 << lvl).astype(BF16)

    zeros8 = jnp.zeros((8, LANES), F32)
    rb = 2 * c
    nrb = seq // rb

    def head_program(hh):
        h = pl.program_id(1) * heads + hh
        cols = slice(hh * GDN_HEAD_DIM, (hh + 1) * GDN_HEAD_DIM)
        xq, xk, xv = xpq.at[hh], xpk.at[hh], xpv.at[hh]
        q_s, k_s, v_s, kk_s, qk_s = qs.at[hh], ks.at[hh], vs.at[hh], kks.at[hh], qks.at[hh]
        aq_s, bo_s, egl_s, o_s, sel_s, row_s = (aqs.at[hh], bos.at[hh], egls.at[hh], osum.at[hh],
                                                selc.at[hh], rowbuf.at[hh])

        l2 = lax.broadcasted_iota(jnp.int32, (2 * c, 2 * c), 0) & (c - 1)
        cb = lax.broadcasted_iota(jnp.int32, (2 * c, 2 * c), 1) >> 7
        lanes_g = (h + LANE_GF, h + LANE_GB)
        lanes_b = (h + LANE_BF, h + LANE_BB)
        for d in range(2):
            sel_s[d] = jnp.where(l2 == jnp.where(cb == 0, lanes_g[d], lanes_b[d]), 1.0, 0.0).astype(BF16)

        sub = lax.broadcasted_iota(jnp.int32, (8, seq), 0)
        for idx, lane0 in enumerate((LANE_GF, LANE_GB, LANE_BB)):
            row_s[idx:idx + 1, :] = jnp.sum(
                jnp.where(sub == h, gct_ref[lane0:lane0 + 8, :], 0.0), axis=0, keepdims=True)
        yield

        def conv_fill(x_ref, xp):
            xp[pl.ds(0, 8), :] = zeros8
            xp[pl.ds(seq + 8, 8), :] = zeros8
            xp[pl.ds(8, seq), :] = x_ref[:, cols].astype(F32)

        def conv_silu(xp, cw_ref, r):
            base = 8 - CONV_K // 2 + r * rb
            acc = cw_ref[0:1, cols] * xp[pl.ds(base, rb), :]
            for j in range(1, CONV_K):
                acc = acc + cw_ref[j:j + 1, cols] * xp[pl.ds(base + j, rb), :]
            return _silu(acc)

        def l2norm(y):
            return y * lax.rsqrt(jnp.sum(y * y, axis=-1, keepdims=True) + NORM_EPS)

        conv_fill(q_ref, xq)
        conv_fill(k_ref, xk)
        conv_fill(v_ref, xv)
        yield
        for r in range(nrb):
            q_s[r * rb:(r + 1) * rb, :] = l2norm(conv_silu(xq, cwq_ref, r)) * (GDN_HEAD_DIM ** -0.5)
            k_s[r * rb:(r + 1) * rb, :] = l2norm(conv_silu(xk, cwk_ref, r))
            for n in range(r * rb // c, (r + 1) * rb // c):
                kcb = k_s[n * c:(n + 1) * c, :].astype(BF16)
                kk_s[n] = _dot_nt(kcb, kcb)
                qk_s[n] = _dot_nt(q_s[n * c:(n + 1) * c, :].astype(BF16), kcb)
            yield
        o_s[...] = jnp.zeros((seq, LANES), F32)
        v_pending = list(range(nrb))

        def conv_v_block():
            r = v_pending.pop(0)
            v_s[r * rb:(r + 1) * rb, :] = conv_silu(xv, cwv_ref, r)

        def row_form(idx, n):
            return jnp.broadcast_to(row_s[idx:idx + 1, n * c:(n + 1) * c], (c, c))

        def prep(trip):
            pairs = [(trip * unroll + u, nc - 1 - trip * unroll - u) for u in range(unroll)]
            work, lps = [], []
            for pair in pairs:
                item = []
                for d, n in enumerate(pair):
                    sel = _dot(gch_ref[n * c:(n + 1) * c, :], sel_s[d])
                    item.append([n, sel[:, :c], sel[:, c:]])
                fw, bw = item
                n, gcol, beta = fw
                decay = jnp.exp(jnp.where(lower, gcol - row_form(0, n), MASKED_LOG))
                l_f = kk_s[n] * decay * beta
                fw.append(qk_s[n] * decay)
                n, gcol, beta = bw
                grow = row_form(1, n)
                decay = jnp.exp(jnp.where(upper, gcol - grow, MASKED_LOG))
                l_b = kk_s[n] * jnp.exp(jnp.where(lower, grow - gcol, MASKED_LOG)) * row_form(2, n)
                bw.append(qk_s[n] * decay)
                lps.append(jnp.concatenate([l_f, l_b], axis=1))
                work.append(item)
                yield
            tps = []
            yield from _unit_tri_inverse_stages(lps, masks, mbf, modd, moddbf, eye_p, tps)
            assert not v_pending
            sols = []
            for item, tp in zip(work, tps):
                for d, (n, gcol, beta, attn) in enumerate(item):
                    t_inv = tp[:, :c] if d == 0 else tp[:, c:].T
                    kc = k_s[n * c:(n + 1) * c, :]
                    eg = jnp.exp(gcol)
                    rhs = jnp.concatenate([(v_s[n * c:(n + 1) * c, :] * beta).astype(BF16),
                                           (kc * beta * eg).astype(BF16)], axis=1)
                    sols.append(_dot(t_inv.astype(BF16), rhs))
            yield
            k = 0
            for item in work:
                for d, (n, gcol, beta, attn) in enumerate(item):
                    kc = k_s[n * c:(n + 1) * c, :]
                    glast = gcol[c - 1:c, :] if d == 0 else gcol[0:1, :]
                    kdt = (kc * jnp.exp(glast - gcol)).T
                    lhs = jnp.concatenate([kdt.astype(BF16), attn.astype(BF16)], axis=0)
                    z = _dot(lhs, sols[k].astype(BF16))
                    k += 1
                    qd = q_s[n * c:(n + 1) * c, :] * jnp.exp(gcol)
                    aq_s[d, n, 0:c, :] = (-z[:c, c:]).astype(BF16)
                    aq_s[d, n, c:2 * c, :] = (qd - z[c:, c:]).astype(BF16)
                    bo_s[d, n] = z[:, :c]
                    egl_s[d, n * 8:(n + 1) * 8, :] = jnp.broadcast_to(jnp.exp(glast), (8, LANES))
            yield

        states = [jnp.zeros((GDN_HEAD_DIM, GDN_HEAD_DIM), F32)] * 2

        def scan_step(trip, u):
            for d in range(2):
                n = trip * unroll + u if d == 0 else nc - 1 - trip * unroll - u
                r = _dot(aq_s[d, n], states[d].astype(BF16)) + bo_s[d, n]
                states[d] = states[d] * egl_s[d, n * 8:n * 8 + 1, :] + r[:c]
                o_s[n * c:(n + 1) * c, :] += r[c:]

        trips = nc // unroll
        active, scan_queue, started, tick = {}, [], 0, 0
        while started < trips or active or scan_queue:
            if started < trips and tick >= started * stagger:
                active[started] = prep(started)
                started += 1
            for trip in sorted(active):
                if next(active[trip], "done") == "done":
                    del active[trip]
                    scan_queue.extend((trip, u) for u in range(unroll))
            if v_pending:
                conv_v_block()
            if scan_queue:
                scan_step(*scan_queue.pop(0))
            tick += 1
            yield

        o = o_s[...]
        y = o * lax.rsqrt(jnp.mean(o * o, axis=-1, keepdims=True) + NORM_EPS) * nw_ref[...]
        o_ref[:, cols] = (y * _silu(z_ref[:, cols].astype(F32))).astype(BF16)

    programs = [head_program(hh) for hh in range(heads)]
    live, tick = set(), 0
    while tick == 0 or live:
        for hh, prog in enumerate(programs):
            if tick == hh * head_offset:
                live.add(hh)
            if hh in live and next(prog, "done") == "done":
                live.discard(hh)
        tick += 1
    assert tick > (heads - 1) * head_offset


def _gdn(pbig, gch, gct, conv_w, norm_w, *, unroll=4, stagger=5, heads=2, head_offset=6):
    b, t, _ = pbig.shape
    c = GDN_CHUNK
    nc = t // c
    hd = GDN_HEAD_DIM
    nh = GDN_HEADS
    hw = heads * hd
    assert nc % unroll == 0 and nh % heads == 0
    nhb = nh // heads
    return pl.pallas_call(
        functools.partial(_gdn_kernel, seq=t, unroll=unroll, stagger=stagger, heads=heads,
                          head_offset=head_offset),
        grid=(b, nhb),
        in_specs=[
            pl.BlockSpec((None, t, hw), lambda i, j: (i, 0, j)),
            pl.BlockSpec((None, t, hw), lambda i, j: (i, 0, nhb + j)),
            pl.BlockSpec((None, t, hw), lambda i, j: (i, 0, 2 * nhb + j)),
            pl.BlockSpec((None, t, hw), lambda i, j: (i, 0, COL_Z // hw + j)),
            pl.BlockSpec((None, t, 2 * LANES), lambda i, j: (i, 0, 0)),
            pl.BlockSpec((None, LANES, t), lambda i, j: (i, 0, 0)),
            pl.BlockSpec((CONV_K, hw), lambda i, j: (0, j)),
            pl.BlockSpec((CONV_K, hw), lambda i, j: (0, nhb + j)),
            pl.BlockSpec((CONV_K, hw), lambda i, j: (0, 2 * nhb + j)),
            pl.BlockSpec((1, hd), lambda i, j: (0, 0)),
        ],
        out_specs=pl.BlockSpec((None, t, hw), lambda i, j: (i, 0, j)),
        out_shape=jax.ShapeDtypeStruct((b, t, GDN_WIDTH), BF16),
        scratch_shapes=[
            pltpu.VMEM((heads, t + 16, hd), F32),
            pltpu.VMEM((heads, t + 16, hd), F32),
            pltpu.VMEM((heads, t + 16, hd), F32),
            pltpu.VMEM((heads, t, hd), F32),
            pltpu.VMEM((heads, t, hd), F32),
            pltpu.VMEM((heads, t, hd), F32),
            pltpu.VMEM((heads, nc, c, c), F32),
            pltpu.VMEM((heads, nc, c, c), F32),
            pltpu.VMEM((heads, 2, nc, 2 * c, hd), BF16),
            pltpu.VMEM((heads, 2, nc, 2 * c, hd), F32),
            pltpu.VMEM((heads, 2, nc * 8, LANES), F32),
            pltpu.VMEM((heads, t, hd), F32),
            pltpu.VMEM((c, 2 * c), F32),
            pltpu.VMEM((GDN_FULL_LEVELS - 1, c, 2 * c), BF16),
            pltpu.VMEM((c // 2, 2 * c), F32),
            pltpu.VMEM((3, c // 2, 2 * c), BF16),
            pltpu.VMEM((heads, 2, 2 * c, 2 * c), BF16),
            pltpu.VMEM((heads, 8, t), F32),
        ],
        compiler_params=pltpu.CompilerParams(
            dimension_semantics=("arbitrary", "arbitrary"), vmem_limit_bytes=VMEM_LIMIT),
        name="gdn",
    )(pbig, pbig, pbig, pbig, gch, gct, conv_w, conv_w, conv_w, norm_w)


def _gla_kernel(q_ref, k_ref, v_ref, g_ref, gc_ref, w2f_ref, w2b_ref, b2f_ref, b2b_ref, nw_ref,
                o_ref, tris, sst, osum, *, seq, unroll):
    c2 = 2 * GLA_CHUNK
    npair = seq // c2
    dk = GLA_HEAD_K
    dv = GLA_HEAD_V
    ri = lax.broadcasted_iota(jnp.int32, (c2, c2), 0)
    ci = lax.broadcasted_iota(jnp.int32, (c2, c2), 1)
    same = (ri >> 6) == (ci >> 6)
    incls = [same & (ri >= ci), same & (ri <= ci)]
    crosses = [(ri >= GLA_CHUNK) & (ci < GLA_CHUNK), (ri < GLA_CHUNK) & (ci >= GLA_CHUNK)]
    row_k = lax.broadcasted_iota(jnp.int32, (c2, dk), 0)
    scanned_first = [row_k < GLA_CHUNK, row_k >= GLA_CHUNK]
    total_rows = [(GLA_CHUNK - 1, c2 - 1), (GLA_CHUNK, 0)]
    w2_refs = (w2f_ref, w2b_ref)
    b2_refs = (b2f_ref, b2b_ref)

    for d in range(2):
        tris[d] = jnp.where(incls[d], 1.0, 0.0).astype(BF16)
    osum[...] = jnp.zeros((seq, dv), F32)
    sst[...] = jnp.zeros((2, dk, dv), F32)

    def pairs(it, _):
        items = []
        for u in range(unroll):
            for d in range(2):
                pp = it * unroll + u
                pp = pp if d == 0 else npair - 1 - pp
                rows = pl.ds(pl.multiple_of(pp * c2, c2), c2)
                pre = _dot(gc_ref[rows, :], w2_refs[d][...]) + b2_refs[d][...]
                items.append([d, rows, pre])
        for x in items:
            gk = _log_sigmoid(x[2]) / GLA_GATE_NORMALIZER
            cs = _dot(tris[x[0]], jnp.concatenate(_split3(gk), axis=1))
            x[2] = cs[:, :dk] + cs[:, dk:2 * dk] + cs[:, 2 * dk:]
        for x in items:
            d, rows, gcum = x
            r1, r2 = total_rows[d]
            tot1, tot2 = gcum[r1:r1 + 1, :], gcum[r2:r2 + 1, :]
            q = q_ref[rows, :].astype(F32) * (dk ** -0.5)
            k = k_ref[rows, :].astype(F32)
            qg = q * jnp.exp(gcum)
            qgb = qg.astype(BF16)
            kg = (k * jnp.exp(-gcum)).astype(BF16)
            kd = k * jnp.exp(jnp.where(scanned_first[d], tot1, tot2) - gcum)
            attn = jnp.where(incls[d], _dot_nt(qgb, kg),
                             jnp.where(crosses[d], _dot_nt(qgb, kd.astype(BF16)), 0.0))
            q_in = jnp.where(scanned_first[d], qg, qg * jnp.exp(tot1)).astype(BF16)
            kd_out = jnp.where(scanned_first[d], kd * jnp.exp(tot2), kd)
            gt = gcum.T
            e_col = jnp.exp(gt[:, r1:r1 + 1] + gt[:, r2:r2 + 1])
            x[2:] = [q_in, attn, kd_out.T.astype(BF16), e_col]
        for x in items:
            d, rows, q_in, attn, kdt, e_col = x
            vb = v_ref[rows, :]
            x[3] = _dot(attn.astype(BF16), vb)
            x[4] = _dot(kdt, vb)
        states = [sst[0], sst[1]]
        for x in items:
            d, rows, q_in, intra, upd, e_col = x
            osum[rows, :] += _dot(q_in, states[d].astype(BF16)) + intra
            states[d] = states[d] * e_col + upd
        sst[0] = states[0]
        sst[1] = states[1]
        return 0

    lax.fori_loop(0, npair // unroll, pairs, 0)

    o = osum[...]
    y = o * lax.rsqrt(jnp.mean(o * o, axis=-1, keepdims=True) + NORM_EPS) * nw_ref[...]
    o_ref[...] = (y * _silu(g_ref[...].astype(F32))).astype(BF16)


def _gla(pbig, gc, w2f, w2b, b2f, b2b, norm_w, *, unroll=4):
    b, t, _ = pbig.shape
    dk = GLA_HEAD_K
    dv = GLA_HEAD_V
    return pl.pallas_call(
        functools.partial(_gla_kernel, seq=t, unroll=unroll),
        grid=(b, GLA_HEADS),
        in_specs=[
            pl.BlockSpec((None, t, dk), lambda i, j: (i, 0, COL_QB // dk + j)),
            pl.BlockSpec((None, t, dk), lambda i, j: (i, 0, COL_KB // dk + j)),
            pl.BlockSpec((None, t, dv), lambda i, j: (i, 0, COL_VB // dv + j)),
            pl.BlockSpec((None, t, dv), lambda i, j: (i, 0, COL_GB // dv + j)),
            pl.BlockSpec((None, t, LANES), lambda i, j: (i, 0, 0)),
            pl.BlockSpec((LANES, dk), lambda i, j: (0, j)),
            pl.BlockSpec((LANES, dk), lambda i, j: (0, j)),
            pl.BlockSpec((1, dk), lambda i, j: (0, j)),
            pl.BlockSpec((1, dk), lambda i, j: (0, j)),
            pl.BlockSpec((1, dv), lambda i, j: (0, 0)),
        ],
        out_specs=pl.BlockSpec((None, t, dv), lambda i, j: (i, 0, j)),
        out_shape=jax.ShapeDtypeStruct((b, t, GLA_VAL_DIM), BF16),
        scratch_shapes=[
            pltpu.VMEM((2, 2 * GLA_CHUNK, 2 * GLA_CHUNK), BF16),
            pltpu.VMEM((2, dk, dv), F32),
            pltpu.VMEM((t, dv), F32),
        ],
        compiler_params=pltpu.CompilerParams(
            dimension_semantics=("arbitrary", "arbitrary"), vmem_limit_bytes=VMEM_LIMIT),
        name="gla",
    )(pbig, pbig, pbig, pbig, gc, w2f, w2b, b2f, b2b, norm_w)


def _merge_kernel(oa_ref, ob_ref, ga_ref, gb_ref, x_ref, wa_ref, wb_ref, wo_ref, lnw_ref, out_ref):
    ya = _dot(oa_ref[...], wa_ref[...])
    yb = _dot(ob_ref[...], wb_ref[...])
    merged = (_sigmoid(ga_ref[...].astype(F32)) * ya + _sigmoid(gb_ref[...].astype(F32)) * yb)
    out = _dot(merged.astype(BF16), wo_ref[...])
    y = out * lax.rsqrt(jnp.mean(out * out, axis=-1, keepdims=True) + NORM_EPS) * lnw_ref[...]
    out_ref[...] = x_ref[...] + y


def _merge(oa, ob, pbig2d, x2d, wa, wb, wo, ln_w, *, tm=512):
    m, d = x2d.shape
    return pl.pallas_call(
        _merge_kernel,
        grid=(m // tm,),
        in_specs=[
            pl.BlockSpec((tm, d), lambda i: (i, 0)),
            pl.BlockSpec((tm, d), lambda i: (i, 0)),
            pl.BlockSpec((tm, d), lambda i: (i, COL_GATE_A // D_MODEL)),
            pl.BlockSpec((tm, d), lambda i: (i, COL_GATE_B // D_MODEL)),
            pl.BlockSpec((tm, d), lambda i: (i, 0)),
            pl.BlockSpec((d, d), lambda i: (0, 0)),
            pl.BlockSpec((d, d), lambda i: (0, 0)),
            pl.BlockSpec((d, d), lambda i: (0, 0)),
            pl.BlockSpec((1, d), lambda i: (0, 0)),
        ],
        out_specs=pl.BlockSpec((tm, d), lambda i: (i, 0)),
        out_shape=jax.ShapeDtypeStruct((m, d), F32),
        compiler_params=pltpu.CompilerParams(
            dimension_semantics=("arbitrary",), vmem_limit_bytes=VMEM_LIMIT),
        name="merge",
    )(oa, ob, pbig2d, pbig2d, x2d, wa, wb, wo, ln_w)


def _layer(x, ln_pre_w, w_in, conv_w, a_log_fwd, a_log_bwd, dt_bias_fwd, dt_bias_bwd,
           gdn_norm_w, w_proj_gdn, gk_w2_fwd, gk_b2_fwd, gk_w2_bwd, gk_b2_bwd,
           gla_norm_w, w_proj_gla, w_out, ln_post_w):
    b, t, d = x.shape
    m = b * t
    o = _OFF
    w_big, w_small = _wprep(w_in.T)
    pad = jnp.zeros((LANES - 2 * GDN_HEADS,), F32)
    alog_vec = jnp.concatenate([a_log_fwd, a_log_bwd, pad]).reshape(1, LANES)
    dtb_vec = jnp.concatenate([dt_bias_fwd, dt_bias_bwd, pad]).reshape(1, LANES)

    def w2_ext(w2, lane0):
        z = jnp.zeros((LANES, GLA_KEY_DIM), F32)
        return z.at[lane0:lane0 + GLA_GATE_RANK, :].set(w2).astype(BF16)

    x2d = x.reshape(m, d)
    pbig, psmall = _inproj(x2d, ln_pre_w.reshape(1, d), w_big, w_small)
    gch, gct = _gates(psmall.reshape(b, t, LANES), alog_vec, dtb_vec)
    pbig3 = pbig.reshape(b, t, N_BIG)
    oa = _gdn(pbig3, gch, gct, conv_w, gdn_norm_w.reshape(1, GDN_HEAD_DIM))
    ob = _gla(pbig3, gch, w2_ext(gk_w2_fwd, LANE_RF), w2_ext(gk_w2_bwd, LANE_RB),
              gk_b2_fwd.reshape(1, GLA_KEY_DIM), gk_b2_bwd.reshape(1, GLA_KEY_DIM),
              gla_norm_w.reshape(1, GLA_HEAD_V))
    out = _merge(oa.reshape(m, GDN_WIDTH), ob.reshape(m, GLA_VAL_DIM), pbig, x2d,
                 w_proj_gdn.astype(BF16), w_proj_gla.astype(BF16), w_out.astype(BF16),
                 ln_post_w.reshape(1, d))
    return out.reshape(b, t, d)


def kernel(x, ln_pre_w, w_in, conv_w, a_log_fwd, a_log_bwd, dt_bias_fwd, dt_bias_bwd, gdn_norm_w,
           w_proj_gdn, gk_w2_fwd, gk_b2_fwd, gk_w2_bwd, gk_b2_bwd, gla_norm_w, w_proj_gla, w_out,
           ln_post_w):
    h = x
    for l in range(ln_pre_w.shape[0]):
        h = _layer(h, ln_pre_w[l], w_in[l], conv_w[l], a_log_fwd[l], a_log_bwd[l],
                   dt_bias_fwd[l], dt_bias_bwd[l], gdn_norm_w[l], w_proj_gdn[l],
                   gk_w2_fwd[l], gk_b2_fwd[l], gk_w2_bwd[l], gk_b2_bwd[l],
                   gla_norm_w[l], w_proj_gla[l], w_out[l], ln_post_w[l])
    return h
```

```python
import functools

import jax
import jax.numpy as jnp
from jax import lax
from jax.experimental import pallas as pl
from jax.experimental.pallas import tpu as pltpu

F32 = jnp.float32
BF16 = jnp.bfloat16

D_MODEL = 1024
GDN_HEADS = 8
GDN_HEAD_DIM = 128
GDN_WIDTH = GDN_HEADS * GDN_HEAD_DIM
CONV_K = 5
GLA_HEADS = 4
GLA_KEY_DIM = D_MODEL // 2
GLA_VAL_DIM = D_MODEL
GLA_HEAD_K = GLA_KEY_DIM // GLA_HEADS
GLA_HEAD_V = GLA_VAL_DIM // GLA_HEADS
GLA_GATE_RANK = 16
GLA_GATE_NORMALIZER = 16.0
GLA_CHUNK = 64
NORM_EPS = 1e-6
MASKED_LOG = -1e30

LANES = 128
GDN_CHUNK = LANES
VMEM_LIMIT = 56 * 1024 * 1024

_IN_SIZES = [3 * GDN_WIDTH, GDN_WIDTH, GDN_HEADS, GDN_HEADS, GDN_HEADS, GDN_HEADS,
             GLA_KEY_DIM, GLA_KEY_DIM, GLA_VAL_DIM, GLA_VAL_DIM,
             GLA_GATE_RANK, GLA_GATE_RANK, D_MODEL, D_MODEL]
_OFF = [0]
for _s in _IN_SIZES:
    _OFF.append(_OFF[-1] + _s)
N_BIG = 3 * GDN_WIDTH + GDN_WIDTH + 2 * GLA_KEY_DIM + 2 * GLA_VAL_DIM + 2 * D_MODEL
COL_Z = 3 * GDN_WIDTH
COL_QB = COL_Z + GDN_WIDTH
COL_KB = COL_QB + GLA_KEY_DIM
COL_VB = COL_KB + GLA_KEY_DIM
COL_GB = COL_VB + GLA_VAL_DIM
COL_GATE_A = COL_GB + GLA_VAL_DIM
COL_GATE_B = COL_GATE_A + D_MODEL
LANE_GF, LANE_GB, LANE_BF, LANE_BB = 0, 8, 16, 24
LANE_RF, LANE_RB = 32, 48


def _dot(a, b):
    return jnp.dot(a, b, preferred_element_type=F32)


def _dot_nt(a, b):
    return lax.dot_general(a, b, (((1,), (1,)), ((), ())), preferred_element_type=F32)


def _split2(x):
    hi = x.astype(BF16)
    lo = (x - hi.astype(F32)).astype(BF16)
    return hi, lo


def _split3(x):
    hi = x.astype(BF16)
    r = x - hi.astype(F32)
    mid = r.astype(BF16)
    lo = (r - mid.astype(F32)).astype(BF16)
    return hi, mid, lo


def _sigmoid(x):
    return jax.nn.sigmoid(x)


def _silu(x):
    return x * _sigmoid(x)


def _softplus(x):
    return jnp.maximum(x, 0.0) + jnp.log(1.0 + jnp.exp(-jnp.abs(x)))


def _log_sigmoid(x):
    return jnp.minimum(x, 0.0) - jnp.log(1.0 + jnp.exp(-jnp.abs(x)))


def _wide_tile_row(j, tn):
    unit = _OFF[6] - _OFF[2]
    assert _OFF[12] - _OFF[10] == unit and _OFF[2] % unit == 0 and _OFF[10] % unit == 0 and tn % unit == 0
    start = j * (tn // unit)
    for first_wide_col in (_OFF[2], _OFF[10] - unit):
        start = start + jnp.where(j * tn >= first_wide_col, 1, 0)
    return start * unit


def _wprep_kernel(w_ref, na_ref, nb_ref, o_ref, osmall_ref):
    o_ref[...] = w_ref[...].astype(BF16)

    @pl.when(pl.program_id(0) == 0)
    def _():
        unit = na_ref.shape[0]
        osmall_ref[0:unit, :] = na_ref[...].astype(BF16)
        osmall_ref[unit:2 * unit, :] = nb_ref[...].astype(BF16)
        osmall_ref[2 * unit:, :] = jnp.zeros((LANES - 2 * unit, osmall_ref.shape[1]), BF16)


def _wprep(w_t, *, tn=1024):
    n, d = w_t.shape
    unit = _OFF[6] - _OFF[2]
    assert all(seg % tn == 0 for seg in (_OFF[2], _OFF[10] - _OFF[6], _OFF[14] - _OFF[12]))
    return pl.pallas_call(
        _wprep_kernel,
        grid=(N_BIG // tn,),
        in_specs=[
            pl.BlockSpec((pl.Element(tn), pl.Element(d)), lambda j: (_wide_tile_row(j, tn), 0)),
            pl.BlockSpec((unit, d), lambda j: (_OFF[2] // unit, 0)),
            pl.BlockSpec((unit, d), lambda j: (_OFF[10] // unit, 0)),
        ],
        out_specs=[
            pl.BlockSpec((tn, d), lambda j: (j, 0)),
            pl.BlockSpec((LANES, d), lambda j: (0, 0)),
        ],
        out_shape=[
            jax.ShapeDtypeStruct((N_BIG, d), BF16),
            jax.ShapeDtypeStruct((LANES, d), BF16),
        ],
        compiler_params=pltpu.CompilerParams(
            dimension_semantics=("arbitrary",), vmem_limit_bytes=VMEM_LIMIT),
        name="wprep",
    )(w_t, w_t, w_t)


def _inproj_kernel(x_ref, lnw_ref, wbig_ref, wsmall_ref, pbig_ref, psmall_ref, h_ref, *, tm):
    j = pl.program_id(1)

    @pl.when(j == 0)
    def _():
        def body(r, _):
            rows = pl.ds(pl.multiple_of(r * LANES, LANES), LANES)
            x = x_ref[rows, :]
            ms = jnp.mean(x * x, axis=-1, keepdims=True)
            h_ref[rows, :] = (x * lax.rsqrt(ms + NORM_EPS) * lnw_ref[...]).astype(BF16)
            return 0
        lax.fori_loop(0, tm // LANES, body, 0)
        psmall_ref[...] = _dot_nt(h_ref[...], wsmall_ref[...])

    pbig_ref[...] = _dot_nt(h_ref[...], wbig_ref[...]).astype(BF16)


def _inproj(x2d, ln_w, w_big, w_small, *, tm=1024, tn=3072):
    m, d = x2d.shape
    return pl.pallas_call(
        functools.partial(_inproj_kernel, tm=tm),
        grid=(m // tm, N_BIG // tn),
        in_specs=[
            pl.BlockSpec((tm, d), lambda i, j: (i, 0)),
            pl.BlockSpec((1, d), lambda i, j: (0, 0)),
            pl.BlockSpec((tn, d), lambda i, j: (j, 0)),
            pl.BlockSpec((LANES, d), lambda i, j: (0, 0)),
        ],
        out_specs=[
            pl.BlockSpec((tm, tn), lambda i, j: (i, j)),
            pl.BlockSpec((tm, LANES), lambda i, j: (i, 0)),
        ],
        out_shape=[
            jax.ShapeDtypeStruct((m, N_BIG), BF16),
            jax.ShapeDtypeStruct((m, LANES), F32),
        ],
        scratch_shapes=[pltpu.VMEM((tm, d), BF16)],
        compiler_params=pltpu.CompilerParams(
            dimension_semantics=("arbitrary", "arbitrary"),
            vmem_limit_bytes=VMEM_LIMIT),
        name="inproj",
    )(x2d, ln_w, w_big, w_small)


def _gates_kernel(ps_ref, alog_ref, dtb_ref, out_ref, out_t_ref, *, seq):
    c = GDN_CHUNK
    ri = lax.broadcasted_iota(jnp.int32, (c, c), 0)
    ci = lax.broadcasted_iota(jnp.int32, (c, c), 1)
    tri_l = jnp.where(ri >= ci, 1.0, 0.0).astype(BF16)
    tri_u = jnp.where(ri <= ci, 1.0, 0.0).astype(BF16)
    lane = lax.broadcasted_iota(jnp.int32, (c, LANES), 1)
    neg_a = -jnp.exp(alog_ref[...])

    for n in range(seq // c):
        a = ps_ref[n * c:(n + 1) * c, :]
        g = neg_a * _softplus(a + dtb_ref[...])
        beta = _sigmoid(a)
        parts = jnp.concatenate(_split3(g), axis=1)
        cl = _dot(tri_l, parts)
        cu = _dot(tri_u, parts)
        cl = cl[:, :LANES] + cl[:, LANES:2 * LANES] + cl[:, 2 * LANES:]
        cu = cu[:, :LANES] + cu[:, LANES:2 * LANES] + cu[:, 2 * LANES:]
        res = jnp.where(
            lane < LANE_GB, cl, jnp.where(lane < LANE_BF, cu, jnp.where(lane < LANE_RF, beta, a)))
        out_ref[n * c:(n + 1) * c, :] = jnp.concatenate(_split2(res), axis=1)
        out_t_ref[:, n * c:(n + 1) * c] = res.T


def _gates(ps, alog_vec, dtb_vec):
    b, t, _ = ps.shape
    return pl.pallas_call(
        functools.partial(_gates_kernel, seq=t),
        grid=(b,),
        in_specs=[
            pl.BlockSpec((None, t, LANES), lambda i: (i, 0, 0)),
            pl.BlockSpec((1, LANES), lambda i: (0, 0)),
            pl.BlockSpec((1, LANES), lambda i: (0, 0)),
        ],
        out_specs=[
            pl.BlockSpec((None, t, 2 * LANES), lambda i: (i, 0, 0)),
            pl.BlockSpec((None, LANES, t), lambda i: (i, 0, 0)),
        ],
        out_shape=[
            jax.ShapeDtypeStruct((b, t, 2 * LANES), BF16),
            jax.ShapeDtypeStruct((b, LANES, t), F32),
        ],
        compiler_params=pltpu.CompilerParams(
            dimension_semantics=("arbitrary",), vmem_limit_bytes=VMEM_LIMIT),
        name="gates",
    )(ps, alog_vec, dtb_vec)


def _block_diag(xp):
    c = xp.shape[0]
    z = jnp.zeros((c, c), xp.dtype)
    return jnp.concatenate([jnp.concatenate([xp[:, :c], z], axis=1),
                            jnp.concatenate([z, xp[:, c:]], axis=1)], axis=0)


def _odd_blocks(x, s):
    return jnp.concatenate([x[b * s:(b + 1) * s] for b in range(1, x.shape[0] // s, 2)], axis=0)


def _spread_odd(x_odd, s):
    z = jnp.zeros((s, x_odd.shape[1]), x_odd.dtype)
    parts = []
    for j in range(x_odd.shape[0] // s):
        parts += [z, x_odd[j * s:(j + 1) * s]]
    return jnp.concatenate(parts, axis=0)


def _sub_from_odd(x, r_odd, s):
    parts = []
    for b in range(x.shape[0] // s):
        blk = x[b * s:(b + 1) * s]
        if b % 2:
            blk = blk - r_odd[(b // 2) * s:(b // 2 + 1) * s]
        parts.append(blk)
    return jnp.concatenate(parts, axis=0)


GDN_FULL_LEVELS = 3
GDN_BF16_ROWS = 16


def _unit_tri_inverse_stages(lps, masks_ref, mbf_ref, modd_ref, moddbf_ref, eye_p, out):
    ts = [eye_p - lp * masks_ref[...] for lp in lps]
    lbs = [lp.astype(BF16) for lp in lps]
    for lvl in range(1, 7):
        s = 1 << lvl
        tbs = [t.astype(BF16) for t in ts]
        if lvl < GDN_FULL_LEVELS:
            ms = [_dot(lb * mbf_ref[lvl - 1], _block_diag(tb)) for lb, tb in zip(lbs, tbs)]
            yield
            ts = [t - _dot(tb, _block_diag(m.astype(BF16))) for t, tb, m in zip(ts, tbs, ms)]
        else:
            if s < GDN_BF16_ROWS:
                cs = [(_odd_blocks(lp, s) * modd_ref[...]).astype(BF16) for lp in lps]
            else:
                mo = moddbf_ref[lvl - GDN_FULL_LEVELS - 1]
                cs = [_odd_blocks(lb, s) * mo for lb in lbs]
            ms = [_dot(cm, _block_diag(tb)) for cm, tb in zip(cs, tbs)]
            yield
            ts = [_sub_from_odd(t, _dot(_odd_blocks(t, s).astype(BF16),
                                        _block_diag(_spread_odd(m, s).astype(BF16))), s)
                  for t, m in zip(ts, ms)]
        yield
    out.extend(ts)


def _gdn_kernel(q_ref, k_ref, v_ref, z_ref, gch_ref, gct_ref, cwq_ref, cwk_ref, cwv_ref, nw_ref,
                o_ref, xpq, xpk, xpv, qs, ks, vs, kks, qks, aqs, bos, egls, osum, masks, mbf, modd,
                moddbf, selc, rowbuf, *, seq, unroll, stagger, heads, head_offset):
    c = GDN_CHUNK
    nc = seq // c
    ri = lax.broadcasted_iota(jnp.int32, (c, c), 0)
    ci = lax.broadcasted_iota(jnp.int32, (c, c), 1)
    lower, upper = ri >= ci, ri <= ci
    rp = lax.broadcasted_iota(jnp.int32, (c, 2 * c), 0)
    cp = lax.broadcasted_iota(jnp.int32, (c, 2 * c), 1) & (c - 1)
    eye_p = jnp.where(rp == cp, 1.0, 0.0)

    for lvl in range(7):
        m = jnp.where(((rp >> (lvl + 1)) == (cp >> (lvl + 1))) & ((rp >> lvl) != (cp >> lvl)), 1.0, 0.0)
        if lvl == 0:
            masks[...] = m
        elif lvl < GDN_FULL_LEVELS:
            mbf[lvl - 1] = m.astype(BF16)
        elif (1 << lvl) < GDN_BF16_ROWS:
            modd[...] = _odd_blocks(m, 1 << lvl)
        else:
            moddbf[lvl - GDN_FULL_LEVELS - 1] = _odd_blocks(m, 1 << lvl).astype(BF16)

    zeros8 = jnp.zeros((8, LANES), F32)
    rb = 2 * c
    nrb = seq // rb

    def head_program(hh):
        h = pl.program_id(1) * heads + hh
        cols = slice(hh * GDN_HEAD_DIM, (hh + 1) * GDN_HEAD_DIM)
        xq, xk, xv = xpq.at[hh], xpk.at[hh], xpv.at[hh]
        q_s, k_s, v_s, kk_s, qk_s = qs.at[hh], ks.at[hh], vs.at[hh], kks.at[hh], qks.at[hh]
        aq_s, bo_s, egl_s, o_s, sel_s, row_s = (aqs.at[hh], bos.at[hh], egls.at[hh], osum.at[hh],
                                                selc.at[hh], rowbuf.at[hh])

        l2 = lax.broadcasted_iota(jnp.int32, (2 * c, 2 * c), 0) & (c - 1)
        cb = lax.broadcasted_iota(jnp.int32, (2 * c, 2 * c), 1) >> 7
        lanes_g = (h + LANE_GF, h + LANE_GB)
        lanes_b = (h + LANE_BF, h + LANE_BB)
        for d in range(2):
            sel_s[d] = jnp.where(l2 == jnp.where(cb == 0, lanes_g[d], lanes_b[d]), 1.0, 0.0).astype(BF16)

        sub = lax.broadcasted_iota(jnp.int32, (8, seq), 0)
        for idx, lane0 in enumerate((LANE_GF, LANE_GB, LANE_BB)):
            row_s[idx:idx + 1, :] = jnp.sum(
                jnp.where(sub == h, gct_ref[lane0:lane0 + 8, :], 0.0), axis=0, keepdims=True)
        yield

        def conv_fill(x_ref, xp):
            xp[pl.ds(0, 8), :] = zeros8
            xp[pl.ds(seq + 8, 8), :] = zeros8
            xp[pl.ds(8, seq), :] = x_ref[:, cols].astype(F32)

        def conv_silu(xp, cw_ref, r):
            base = 8 - CONV_K // 2 + r * rb
            acc = cw_ref[0:1, cols] * xp[pl.ds(base, rb), :]
            for j in range(1, CONV_K):
                acc = acc + cw_ref[j:j + 1, cols] * xp[pl.ds(base + j, rb), :]
            return _silu(acc)

        def l2norm(y):
            return y * lax.rsqrt(jnp.sum(y * y, axis=-1, keepdims=True) + NORM_EPS)

        conv_fill(q_ref, xq)
        conv_fill(k_ref, xk)
        conv_fill(v_ref, xv)
        yield
        for r in range(nrb):
            q_s[r * rb:(r + 1) * rb, :] = l2norm(conv_silu(xq, cwq_ref, r)) * (GDN_HEAD_DIM ** -0.5)
            k_s[r * rb:(r + 1) * rb, :] = l2norm(conv_silu(xk, cwk_ref, r))
            for n in range(r * rb // c, (r + 1) * rb // c):
                kcb = k_s[n * c:(n + 1) * c, :].astype(BF16)
                kk_s[n] = _dot_nt(kcb, kcb)
                qk_s[n] = _dot_nt(q_s[n * c:(n + 1) * c, :].astype(BF16), kcb)
            yield
        o_s[...] = jnp.zeros((seq, LANES), F32)
        v_pending = list(range(nrb))

        def conv_v_block():
            r = v_pending.pop(0)
            v_s[r * rb:(r + 1) * rb, :] = conv_silu(xv, cwv_ref, r)

        def row_form(idx, n):
            return jnp.broadcast_to(row_s[idx:idx + 1, n * c:(n + 1) * c], (c, c))

        def prep(trip):
            pairs = [(trip * unroll + u, nc - 1 - trip * unroll - u) for u in range(unroll)]
            work, lps = [], []
            for pair in pairs:
                item = []
                for d, n in enumerate(pair):
                    sel = _dot(gch_ref[n * c:(n + 1) * c, :], sel_s[d])
                    item.append([n, sel[:, :c], sel[:, c:]])
                fw, bw = item
                n, gcol, beta = fw
                decay = jnp.exp(jnp.where(lower, gcol - row_form(0, n), MASKED_LOG))
                l_f = kk_s[n] * decay * beta
                fw.append(qk_s[n] * decay)
                n, gcol, beta = bw
                grow = row_form(1, n)
                decay = jnp.exp(jnp.where(upper, gcol - grow, MASKED_LOG))
                l_b = kk_s[n] * jnp.exp(jnp.where(lower, grow - gcol, MASKED_LOG)) * row_form(2, n)
                bw.append(qk_s[n] * decay)
                lps.append(jnp.concatenate([l_f, l_b], axis=1))
                work.append(item)
                yield
            tps = []
            yield from _unit_tri_inverse_stages(lps, masks, mbf, modd, moddbf, eye_p, tps)
            assert not v_pending
            sols = []
            for item, tp in zip(work, tps):
                for d, (n, gcol, beta, attn) in enumerate(item):
                    t_inv = tp[:, :c] if d == 0 else tp[:, c:].T
                    kc = k_s[n * c:(n + 1) * c, :]
                    eg = jnp.exp(gcol)
                    rhs = jnp.concatenate([(v_s[n * c:(n + 1) * c, :] * beta).astype(BF16),
                                           (kc * beta * eg).astype(BF16)], axis=1)
                    sols.append(_dot(t_inv.astype(BF16), rhs))
            yield
            k = 0
            for item in work:
                for d, (n, gcol, beta, attn) in enumerate(item):
                    kc = k_s[n * c:(n + 1) * c, :]
                    glast = gcol[c - 1:c, :] if d == 0 else gcol[0:1, :]
                    kdt = (kc * jnp.exp(glast - gcol)).T
                    lhs = jnp.concatenate([kdt.astype(BF16), attn.astype(BF16)], axis=0)
                    z = _dot(lhs, sols[k].astype(BF16))
                    k += 1
                    qd = q_s[n * c:(n + 1) * c, :] * jnp.exp(gcol)
                    aq_s[d, n, 0:c, :] = (-z[:c, c:]).astype(BF16)
                    aq_s[d, n, c:2 * c, :] = (qd - z[c:, c:]).astype(BF16)
                    bo_s[d, n] = z[:, :c]
                    egl_s[d, n * 8:(n + 1) * 8, :] = jnp.broadcast_to(jnp.exp(glast), (8, LANES))
            yield

        states = [jnp.zeros((GDN_HEAD_DIM, GDN_HEAD_DIM), F32)] * 2

        def scan_step(trip, u):
            for d in range(2):
                n = trip * unroll + u if d == 0 else nc - 1 - trip * unroll - u
                r = _dot(aq_s[d, n], states[d].astype(BF16)) + bo_s[d, n]
                states[d] = states[d] * egl_s[d, n * 8:n * 8 + 1, :] + r[:c]
                o_s[n * c:(n + 1) * c, :] += r[c:]

        trips = nc // unroll
        active, scan_queue, started, tick = {}, [], 0, 0
        while started < trips or active or scan_queue:
            if started < trips and tick >= started * stagger:
                active[started] = prep(started)
                started += 1
            for trip in sorted(active):
                if next(active[trip], "done") == "done":
                    del active[trip]
                    scan_queue.extend((trip, u) for u in range(unroll))
            if v_pending:
                conv_v_block()
            if scan_queue:
                scan_step(*scan_queue.pop(0))
            tick += 1
            yield

        o = o_s[...]
        y = o * lax.rsqrt(jnp.mean(o * o, axis=-1, keepdims=True) + NORM_EPS) * nw_ref[...]
        o_ref[:, cols] = (y * _silu(z_ref[:, cols].astype(F32))).astype(BF16)

    programs = [head_program(hh) for hh in range(heads)]
    live, tick = set(), 0
    while tick == 0 or live:
        for hh, prog in enumerate(programs):
            if tick == hh * head_offset:
                live.add(hh)
            if hh in live and next(prog, "done") == "done":
                live.discard(hh)
        tick += 1
    assert tick > (heads - 1) * head_offset


def _gdn(pbig, gch, gct, conv_w, norm_w, *, unroll=4, stagger=5, heads=2, head_offset=3):
    b, t, _ = pbig.shape
    c = GDN_CHUNK
    nc = t // c
    hd = GDN_HEAD_DIM
    nh = GDN_HEADS
    hw = heads * hd
    assert nc % unroll == 0 and nh % heads == 0
    nhb = nh // heads
    return pl.pallas_call(
        functools.partial(_gdn_kernel, seq=t, unroll=unroll, stagger=stagger, heads=heads,
                          head_offset=head_offset),
        grid=(b, nhb),
        in_specs=[
            pl.BlockSpec((None, t, hw), lambda i, j: (i, 0, j)),
            pl.BlockSpec((None, t, hw), lambda i, j: (i, 0, nhb + j)),
            pl.BlockSpec((None, t, hw), lambda i, j: (i, 0, 2 * nhb + j)),
            pl.BlockSpec((None, t, hw), lambda i, j: (i, 0, COL_Z // hw + j)),
            pl.BlockSpec((None, t, 2 * LANES), lambda i, j: (i, 0, 0)),
            pl.BlockSpec((None, LANES, t), lambda i, j: (i, 0, 0)),
            pl.BlockSpec((CONV_K, hw), lambda i, j: (0, j)),
            pl.BlockSpec((CONV_K, hw), lambda i, j: (0, nhb + j)),
            pl.BlockSpec((CONV_K, hw), lambda i, j: (0, 2 * nhb + j)),
            pl.BlockSpec((1, hd), lambda i, j: (0, 0)),
        ],
        out_specs=pl.BlockSpec((None, t, hw), lambda i, j: (i, 0, j)),
        out_shape=jax.ShapeDtypeStruct((b, t, GDN_WIDTH), BF16),
        scratch_shapes=[
            pltpu.VMEM((heads, t + 16, hd), F32),
            pltpu.VMEM((heads, t + 16, hd), F32),
            pltpu.VMEM((heads, t + 16, hd), F32),
            pltpu.VMEM((heads, t, hd), F32),
            pltpu.VMEM((heads, t, hd), F32),
            pltpu.VMEM((heads, t, hd), F32),
            pltpu.VMEM((heads, nc, c, c), F32),
            pltpu.VMEM((heads, nc, c, c), F32),
            pltpu.VMEM((heads, 2, nc, 2 * c, hd), BF16),
            pltpu.VMEM((heads, 2, nc, 2 * c, hd), F32),
            pltpu.VMEM((heads, 2, nc * 8, LANES), F32),
            pltpu.VMEM((heads, t, hd), F32),
            pltpu.VMEM((c, 2 * c), F32),
            pltpu.VMEM((GDN_FULL_LEVELS - 1, c, 2 * c), BF16),
            pltpu.VMEM((c // 2, 2 * c), F32),
            pltpu.VMEM((3, c // 2, 2 * c), BF16),
            pltpu.VMEM((heads, 2, 2 * c, 2 * c), BF16),
            pltpu.VMEM((heads, 8, t), F32),
        ],
        compiler_params=pltpu.CompilerParams(
            dimension_semantics=("arbitrary", "arbitrary"), vmem_limit_bytes=VMEM_LIMIT),
        name="gdn",
    )(pbig, pbig, pbig, pbig, gch, gct, conv_w, conv_w, conv_w, norm_w)


def _gla_kernel(q_ref, k_ref, v_ref, g_ref, gc_ref, w2f_ref, w2b_ref, b2f_ref, b2b_ref, nw_ref,
                o_ref, tris, sst, osum, *, seq, unroll, heads):
    c2 = 2 * GLA_CHUNK
    npair = seq // c2
    dk = GLA_HEAD_K
    dv = GLA_HEAD_V
    ri = lax.broadcasted_iota(jnp.int32, (c2, c2), 0)
    ci = lax.broadcasted_iota(jnp.int32, (c2, c2), 1)
    same = (ri >> 6) == (ci >> 6)
    incls = [same & (ri >= ci), same & (ri <= ci)]
    crosses = [(ri >= GLA_CHUNK) & (ci < GLA_CHUNK), (ri < GLA_CHUNK) & (ci >= GLA_CHUNK)]
    row_k = lax.broadcasted_iota(jnp.int32, (c2, dk), 0)
    scanned_first = [row_k < GLA_CHUNK, row_k >= GLA_CHUNK]
    total_rows = [(GLA_CHUNK - 1, c2 - 1), (GLA_CHUNK, 0)]
    w2_refs = (w2f_ref, w2b_ref)
    b2_refs = (b2f_ref, b2b_ref)

    for d in range(2):
        tris[d] = jnp.where(incls[d], 1.0, 0.0).astype(BF16)
    osum[...] = jnp.zeros((seq, heads * dv), F32)
    sst[...] = jnp.zeros((heads * 2, dk, dv), F32)

    def pairs(it, _):
        items = []
        for u in range(unroll):
            for hh in range(heads):
                for d in range(2):
                    pp = it * unroll + u
                    pp = pp if d == 0 else npair - 1 - pp
                    rows = pl.ds(pl.multiple_of(pp * c2, c2), c2)
                    kc = slice(hh * dk, (hh + 1) * dk)
                    pre = _dot(gc_ref[rows, :], w2_refs[d][:, kc]) + b2_refs[d][:, kc]
                    items.append([(hh, d), rows, pre])
        for x in items:
            gk = _log_sigmoid(x[2]) / GLA_GATE_NORMALIZER
            cs = _dot(tris[x[0][1]], jnp.concatenate(_split3(gk), axis=1))
            x[2] = cs[:, :dk] + cs[:, dk:2 * dk] + cs[:, 2 * dk:]
        for x in items:
            (hh, d), rows, gcum = x
            kc = slice(hh * dk, (hh + 1) * dk)
            r1, r2 = total_rows[d]
            tot1, tot2 = gcum[r1:r1 + 1, :], gcum[r2:r2 + 1, :]
            q = q_ref[rows, kc].astype(F32) * (dk ** -0.5)
            k = k_ref[rows, kc].astype(F32)
            qg = q * jnp.exp(gcum)
            qgb = qg.astype(BF16)
            kg = (k * jnp.exp(-gcum)).astype(BF16)
            kd = k * jnp.exp(jnp.where(scanned_first[d], tot1, tot2) - gcum)
            attn = jnp.where(incls[d], _dot_nt(qgb, kg),
                             jnp.where(crosses[d], _dot_nt(qgb, kd.astype(BF16)), 0.0))
            q_in = jnp.where(scanned_first[d], qg, qg * jnp.exp(tot1)).astype(BF16)
            kd_out = jnp.where(scanned_first[d], kd * jnp.exp(tot2), kd)
            gt = gcum.T
            e_col = jnp.exp(gt[:, r1:r1 + 1] + gt[:, r2:r2 + 1])
            x[2:] = [q_in, attn, kd_out.T.astype(BF16), e_col]
        for x in items:
            (hh, d), rows, q_in, attn, kdt, e_col = x
            vb = v_ref[rows, hh * dv:(hh + 1) * dv]
            x[3] = _dot(attn.astype(BF16), vb)
            x[4] = _dot(kdt, vb)
        states = [sst[i] for i in range(heads * 2)]
        for x in items:
            (hh, d), rows, q_in, intra, upd, e_col = x
            osum[rows, hh * dv:(hh + 1) * dv] += _dot(q_in, states[hh * 2 + d].astype(BF16)) + intra
            states[hh * 2 + d] = states[hh * 2 + d] * e_col + upd
        for i in range(heads * 2):
            sst[i] = states[i]
        return 0

    lax.fori_loop(0, npair // unroll, pairs, 0)

    for hh in range(heads):
        vc = slice(hh * dv, (hh + 1) * dv)
        o = osum[:, vc]
        y = o * lax.rsqrt(jnp.mean(o * o, axis=-1, keepdims=True) + NORM_EPS) * nw_ref[...]
        o_ref[:, vc] = (y * _silu(g_ref[:, vc].astype(F32))).astype(BF16)


def _gla(pbig, gc, w2f, w2b, b2f, b2b, norm_w, *, unroll=4, heads=2):
    b, t, _ = pbig.shape
    dk = GLA_HEAD_K
    dv = GLA_HEAD_V
    kw, vw = heads * dk, heads * dv
    assert GLA_HEADS % heads == 0
    return pl.pallas_call(
        functools.partial(_gla_kernel, seq=t, unroll=unroll, heads=heads),
        grid=(b, GLA_HEADS // heads),
        in_specs=[
            pl.BlockSpec((None, t, kw), lambda i, j: (i, 0, COL_QB // kw + j)),
            pl.BlockSpec((None, t, kw), lambda i, j: (i, 0, COL_KB // kw + j)),
            pl.BlockSpec((None, t, vw), lambda i, j: (i, 0, COL_VB // vw + j)),
            pl.BlockSpec((None, t, vw), lambda i, j: (i, 0, COL_GB // vw + j)),
            pl.BlockSpec((None, t, LANES), lambda i, j: (i, 0, 0)),
            pl.BlockSpec((LANES, kw), lambda i, j: (0, j)),
            pl.BlockSpec((LANES, kw), lambda i, j: (0, j)),
            pl.BlockSpec((1, kw), lambda i, j: (0, j)),
            pl.BlockSpec((1, kw), lambda i, j: (0, j)),
            pl.BlockSpec((1, dv), lambda i, j: (0, 0)),
        ],
        out_specs=pl.BlockSpec((None, t, vw), lambda i, j: (i, 0, j)),
        out_shape=jax.ShapeDtypeStruct((b, t, GLA_VAL_DIM), BF16),
        scratch_shapes=[
            pltpu.VMEM((2, 2 * GLA_CHUNK, 2 * GLA_CHUNK), BF16),
            pltpu.VMEM((heads * 2, dk, dv), F32),
            pltpu.VMEM((t, vw), F32),
        ],
        compiler_params=pltpu.CompilerParams(
            dimension_semantics=("arbitrary", "arbitrary"), vmem_limit_bytes=VMEM_LIMIT),
        name="gla",
    )(pbig, pbig, pbig, pbig, gc, w2f, w2b, b2f, b2b, norm_w)


def _merge_kernel(oa_ref, ob_ref, ga_ref, gb_ref, x_ref, wa_ref, wb_ref, wo_ref, lnw_ref, out_ref):
    ya = _dot(oa_ref[...], wa_ref[...])
    yb = _dot(ob_ref[...], wb_ref[...])
    merged = (_sigmoid(ga_ref[...].astype(F32)) * ya + _sigmoid(gb_ref[...].astype(F32)) * yb)
    out = _dot(merged.astype(BF16), wo_ref[...])
    y = out * lax.rsqrt(jnp.mean(out * out, axis=-1, keepdims=True) + NORM_EPS) * lnw_ref[...]
    out_ref[...] = x_ref[...] + y


def _merge(oa, ob, pbig2d, x2d, wa, wb, wo, ln_w, *, tm=512):
    m, d = x2d.shape
    return pl.pallas_call(
        _merge_kernel,
        grid=(m // tm,),
        in_specs=[
            pl.BlockSpec((tm, d), lambda i: (i, 0)),
            pl.BlockSpec((tm, d), lambda i: (i, 0)),
            pl.BlockSpec((tm, d), lambda i: (i, COL_GATE_A // D_MODEL)),
            pl.BlockSpec((tm, d), lambda i: (i, COL_GATE_B // D_MODEL)),
            pl.BlockSpec((tm, d), lambda i: (i, 0)),
            pl.BlockSpec((d, d), lambda i: (0, 0)),
            pl.BlockSpec((d, d), lambda i: (0, 0)),
            pl.BlockSpec((d, d), lambda i: (0, 0)),
            pl.BlockSpec((1, d), lambda i: (0, 0)),
        ],
        out_specs=pl.BlockSpec((tm, d), lambda i: (i, 0)),
        out_shape=jax.ShapeDtypeStruct((m, d), F32),
        compiler_params=pltpu.CompilerParams(
            dimension_semantics=("arbitrary",), vmem_limit_bytes=VMEM_LIMIT),
        name="merge",
    )(oa, ob, pbig2d, pbig2d, x2d, wa, wb, wo, ln_w)


def _layer(x, ln_pre_w, w_in, conv_w, a_log_fwd, a_log_bwd, dt_bias_fwd, dt_bias_bwd,
           gdn_norm_w, w_proj_gdn, gk_w2_fwd, gk_b2_fwd, gk_w2_bwd, gk_b2_bwd,
           gla_norm_w, w_proj_gla, w_out, ln_post_w):
    b, t, d = x.shape
    m = b * t
    o = _OFF
    w_big, w_small = _wprep(w_in.T)
    pad = jnp.zeros((LANES - 2 * GDN_HEADS,), F32)
    alog_vec = jnp.concatenate([a_log_fwd, a_log_bwd, pad]).reshape(1, LANES)
    dtb_vec = jnp.concatenate([dt_bias_fwd, dt_bias_bwd, pad]).reshape(1, LANES)

    def w2_ext(w2, lane0):
        z = jnp.zeros((LANES, GLA_KEY_DIM), F32)
        return z.at[lane0:lane0 + GLA_GATE_RANK, :].set(w2).astype(BF16)

    x2d = x.reshape(m, d)
    pbig, psmall = _inproj(x2d, ln_pre_w.reshape(1, d), w_big, w_small)
    gch, gct = _gates(psmall.reshape(b, t, LANES), alog_vec, dtb_vec)
    pbig3 = pbig.reshape(b, t, N_BIG)
    oa = _gdn(pbig3, gch, gct, conv_w, gdn_norm_w.reshape(1, GDN_HEAD_DIM))
    ob = _gla(pbig3, gch, w2_ext(gk_w2_fwd, LANE_RF), w2_ext(gk_w2_bwd, LANE_RB),
              gk_b2_fwd.reshape(1, GLA_KEY_DIM), gk_b2_bwd.reshape(1, GLA_KEY_DIM),
              gla_norm_w.reshape(1, GLA_HEAD_V))
    out = _merge(oa.reshape(m, GDN_WIDTH), ob.reshape(m, GLA_VAL_DIM), pbig, x2d,
                 w_proj_gdn.astype(BF16), w_proj_gla.astype(BF16), w_out.astype(BF16),
                 ln_post_w.reshape(1, d))
    return out.reshape(b, t, d)


def kernel(x, ln_pre_w, w_in, conv_w, a_log_fwd, a_log_bwd, dt_bias_fwd, dt_bias_bwd, gdn_norm_w,
           w_proj_gdn, gk_w2_fwd, gk_b2_fwd, gk_w2_bwd, gk_b2_bwd, gla_norm_w, w_proj_gla, w_out,
           ln_post_w):
    h = x
    for l in range(ln_pre_w.shape[0]):
        h = _layer(h, ln_pre_w[l], w_in[l], conv_w[l], a_log_fwd[l], a_log_bwd[l],
                   dt_bias_fwd[l], dt_bias_bwd[l], gdn_norm_w[l], w_proj_gdn[l],
                   gk_w2_fwd[l], gk_b2_fwd[l], gk_w2_bwd[l], gk_b2_bwd[l],
                   gla_norm_w[l], w_proj_gla[l], w_out[l], ln_post_w[l])
    return h
```

```python
import functools

import jax
import jax.numpy as jnp
from jax import lax
from jax.experimental import pallas as pl
from jax.experimental.pallas import tpu as pltpu

F32 = jnp.float32
BF16 = jnp.bfloat16

D_MODEL = 1024
GDN_HEADS = 8
GDN_HEAD_DIM = 128
GDN_WIDTH = GDN_HEADS * GDN_HEAD_DIM
CONV_K = 5
GLA_HEADS = 4
GLA_KEY_DIM = D_MODEL // 2
GLA_VAL_DIM = D_MODEL
GLA_HEAD_K = GLA_KEY_DIM // GLA_HEADS
GLA_HEAD_V = GLA_VAL_DIM // GLA_HEADS
GLA_GATE_RANK = 16
GLA_GATE_NORMALIZER = 16.0
GLA_CHUNK = 64
NORM_EPS = 1e-6
MASKED_LOG = -1e30

LANES = 128
GDN_CHUNK = LANES
VMEM_LIMIT = 56 * 1024 * 1024

_IN_SIZES = [3 * GDN_WIDTH, GDN_WIDTH, GDN_HEADS, GDN_HEADS, GDN_HEADS, GDN_HEADS,
             GLA_KEY_DIM, GLA_KEY_DIM, GLA_VAL_DIM, GLA_VAL_DIM,
             GLA_GATE_RANK, GLA_GATE_RANK, D_MODEL, D_MODEL]
_OFF = [0]
for _s in _IN_SIZES:
    _OFF.append(_OFF[-1] + _s)
N_BIG = 3 * GDN_WIDTH + GDN_WIDTH + 2 * GLA_KEY_DIM + 2 * GLA_VAL_DIM + 2 * D_MODEL
COL_Z = 3 * GDN_WIDTH
COL_QB = COL_Z + GDN_WIDTH
COL_KB = COL_QB + GLA_KEY_DIM
COL_VB = COL_KB + GLA_KEY_DIM
COL_GB = COL_VB + GLA_VAL_DIM
COL_GATE_A = COL_GB + GLA_VAL_DIM
COL_GATE_B = COL_GATE_A + D_MODEL
LANE_GF, LANE_GB, LANE_BF, LANE_BB = 0, 8, 16, 24
LANE_RF, LANE_RB = 32, 48


def _dot(a, b):
    return jnp.dot(a, b, preferred_element_type=F32)


def _dot_nt(a, b):
    return lax.dot_general(a, b, (((1,), (1,)), ((), ())), preferred_element_type=F32)


def _split2(x):
    hi = x.astype(BF16)
    lo = (x - hi.astype(F32)).astype(BF16)
    return hi, lo


def _split3(x):
    hi = x.astype(BF16)
    r = x - hi.astype(F32)
    mid = r.astype(BF16)
    lo = (r - mid.astype(F32)).astype(BF16)
    return hi, mid, lo


def _sigmoid(x):
    return jax.nn.sigmoid(x)


def _silu(x):
    return x * _sigmoid(x)


def _softplus(x):
    return jnp.maximum(x, 0.0) + jnp.log(1.0 + jnp.exp(-jnp.abs(x)))


def _log_sigmoid(x):
    return jnp.minimum(x, 0.0) - jnp.log(1.0 + jnp.exp(-jnp.abs(x)))


def _wide_tile_row(j, tn):
    unit = _OFF[6] - _OFF[2]
    assert _OFF[12] - _OFF[10] == unit and _OFF[2] % unit == 0 and _OFF[10] % unit == 0 and tn % unit == 0
    start = j * (tn // unit)
    for first_wide_col in (_OFF[2], _OFF[10] - unit):
        start = start + jnp.where(j * tn >= first_wide_col, 1, 0)
    return start * unit


def _wprep_kernel(w_ref, na_ref, nb_ref, o_ref, osmall_ref):
    o_ref[...] = w_ref[...].astype(BF16)

    @pl.when(pl.program_id(0) == 0)
    def _():
        unit = na_ref.shape[0]
        osmall_ref[0:unit, :] = na_ref[...].astype(BF16)
        osmall_ref[unit:2 * unit, :] = nb_ref[...].astype(BF16)
        osmall_ref[2 * unit:, :] = jnp.zeros((LANES - 2 * unit, osmall_ref.shape[1]), BF16)


def _wprep(w_t, *, tn=1024):
    n, d = w_t.shape
    unit = _OFF[6] - _OFF[2]
    assert all(seg % tn == 0 for seg in (_OFF[2], _OFF[10] - _OFF[6], _OFF[14] - _OFF[12]))
    return pl.pallas_call(
        _wprep_kernel,
        grid=(N_BIG // tn,),
        in_specs=[
            pl.BlockSpec((pl.Element(tn), pl.Element(d)), lambda j: (_wide_tile_row(j, tn), 0)),
            pl.BlockSpec((unit, d), lambda j: (_OFF[2] // unit, 0)),
            pl.BlockSpec((unit, d), lambda j: (_OFF[10] // unit, 0)),
        ],
        out_specs=[
            pl.BlockSpec((tn, d), lambda j: (j, 0)),
            pl.BlockSpec((LANES, d), lambda j: (0, 0)),
        ],
        out_shape=[
            jax.ShapeDtypeStruct((N_BIG, d), BF16),
            jax.ShapeDtypeStruct((LANES, d), BF16),
        ],
        compiler_params=pltpu.CompilerParams(
            dimension_semantics=("arbitrary",), vmem_limit_bytes=VMEM_LIMIT),
        name="wprep",
    )(w_t, w_t, w_t)


def _inproj_kernel(x_ref, lnw_ref, wbig_ref, wsmall_ref, pbig_ref, psmall_ref, h_ref, *, tm):
    j = pl.program_id(1)

    @pl.when(j == 0)
    def _():
        def body(r, _):
            rows = pl.ds(pl.multiple_of(r * LANES, LANES), LANES)
            x = x_ref[rows, :]
            ms = jnp.mean(x * x, axis=-1, keepdims=True)
            h_ref[rows, :] = (x * lax.rsqrt(ms + NORM_EPS) * lnw_ref[...]).astype(BF16)
            return 0
        lax.fori_loop(0, tm // LANES, body, 0)
        psmall_ref[...] = _dot_nt(h_ref[...], wsmall_ref[...])

    pbig_ref[...] = _dot_nt(h_ref[...], wbig_ref[...]).astype(BF16)


def _inproj(x2d, ln_w, w_big, w_small, *, tm=1024, tn=3072):
    m, d = x2d.shape
    return pl.pallas_call(
        functools.partial(_inproj_kernel, tm=tm),
        grid=(m // tm, N_BIG // tn),
        in_specs=[
            pl.BlockSpec((tm, d), lambda i, j: (i, 0)),
            pl.BlockSpec((1, d), lambda i, j: (0, 0)),
            pl.BlockSpec((tn, d), lambda i, j: (j, 0)),
            pl.BlockSpec((LANES, d), lambda i, j: (0, 0)),
        ],
        out_specs=[
            pl.BlockSpec((tm, tn), lambda i, j: (i, j)),
            pl.BlockSpec((tm, LANES), lambda i, j: (i, 0)),
        ],
        out_shape=[
            jax.ShapeDtypeStruct((m, N_BIG), BF16),
            jax.ShapeDtypeStruct((m, LANES), F32),
        ],
        scratch_shapes=[pltpu.VMEM((tm, d), BF16)],
        compiler_params=pltpu.CompilerParams(
            dimension_semantics=("arbitrary", "arbitrary"),
            vmem_limit_bytes=VMEM_LIMIT),
        name="inproj",
    )(x2d, ln_w, w_big, w_small)


def _gates_kernel(ps_ref, alog_ref, dtb_ref, out_ref, out_t_ref, *, seq):
    c = GDN_CHUNK
    ri = lax.broadcasted_iota(jnp.int32, (c, c), 0)
    ci = lax.broadcasted_iota(jnp.int32, (c, c), 1)
    tri_l = jnp.where(ri >= ci, 1.0, 0.0).astype(BF16)
    tri_u = jnp.where(ri <= ci, 1.0, 0.0).astype(BF16)
    lane = lax.broadcasted_iota(jnp.int32, (c, LANES), 1)
    neg_a = -jnp.exp(alog_ref[...])

    for n in range(seq // c):
        a = ps_ref[n * c:(n + 1) * c, :]
        g = neg_a * _softplus(a + dtb_ref[...])
        beta = _sigmoid(a)
        parts = jnp.concatenate(_split3(g), axis=1)
        cl = _dot(tri_l, parts)
        cu = _dot(tri_u, parts)
        cl = cl[:, :LANES] + cl[:, LANES:2 * LANES] + cl[:, 2 * LANES:]
        cu = cu[:, :LANES] + cu[:, LANES:2 * LANES] + cu[:, 2 * LANES:]
        res = jnp.where(
            lane < LANE_GB, cl, jnp.where(lane < LANE_BF, cu, jnp.where(lane < LANE_RF, beta, a)))
        out_ref[n * c:(n + 1) * c, :] = jnp.concatenate(_split2(res), axis=1)
        out_t_ref[:, n * c:(n + 1) * c] = res.T


def _gates(ps, alog_vec, dtb_vec):
    b, t, _ = ps.shape
    return pl.pallas_call(
        functools.partial(_gates_kernel, seq=t),
        grid=(b,),
        in_specs=[
            pl.BlockSpec((None, t, LANES), lambda i: (i, 0, 0)),
            pl.BlockSpec((1, LANES), lambda i: (0, 0)),
            pl.BlockSpec((1, LANES), lambda i: (0, 0)),
        ],
        out_specs=[
            pl.BlockSpec((None, t, 2 * LANES), lambda i: (i, 0, 0)),
            pl.BlockSpec((None, LANES, t), lambda i: (i, 0, 0)),
        ],
        out_shape=[
            jax.ShapeDtypeStruct((b, t, 2 * LANES), BF16),
            jax.ShapeDtypeStruct((b, LANES, t), F32),
        ],
        compiler_params=pltpu.CompilerParams(
            dimension_semantics=("arbitrary",), vmem_limit_bytes=VMEM_LIMIT),
        name="gates",
    )(ps, alog_vec, dtb_vec)


def _block_diag(xp):
    c = xp.shape[0]
    z = jnp.zeros((c, c), xp.dtype)
    return jnp.concatenate([jnp.concatenate([xp[:, :c], z], axis=1),
                            jnp.concatenate([z, xp[:, c:]], axis=1)], axis=0)


def _odd_blocks(x, s):
    return jnp.concatenate([x[b * s:(b + 1) * s] for b in range(1, x.shape[0] // s, 2)], axis=0)


def _spread_odd(x_odd, s):
    z = jnp.zeros((s, x_odd.shape[1]), x_odd.dtype)
    parts = []
    for j in range(x_odd.shape[0] // s):
        parts += [z, x_odd[j * s:(j + 1) * s]]
    return jnp.concatenate(parts, axis=0)


def _sub_from_odd(x, r_odd, s):
    parts = []
    for b in range(x.shape[0] // s):
        blk = x[b * s:(b + 1) * s]
        if b % 2:
            blk = blk - r_odd[(b // 2) * s:(b // 2 + 1) * s]
        parts.append(blk)
    return jnp.concatenate(parts, axis=0)


GDN_FULL_LEVELS = 3
GDN_BF16_ROWS = 16


def _unit_tri_inverse_stages(lps, masks_ref, mbf_ref, modd_ref, moddbf_ref, eye_p, out):
    ts = [eye_p - lp * masks_ref[...] for lp in lps]
    lbs = [lp.astype(BF16) for lp in lps]
    for lvl in range(1, 7):
        s = 1 << lvl
        tbs = [t.astype(BF16) for t in ts]
        if lvl < GDN_FULL_LEVELS:
            ms = [_dot(lb * mbf_ref[lvl - 1], _block_diag(tb)) for lb, tb in zip(lbs, tbs)]
            yield
            ts = [t - _dot(tb, _block_diag(m.astype(BF16))) for t, tb, m in zip(ts, tbs, ms)]
        else:
            if s < GDN_BF16_ROWS:
                cs = [(_odd_blocks(lp, s) * modd_ref[...]).astype(BF16) for lp in lps]
            else:
                mo = moddbf_ref[lvl - GDN_FULL_LEVELS - 1]
                cs = [_odd_blocks(lb, s) * mo for lb in lbs]
            ms = [_dot(cm, _block_diag(tb)) for cm, tb in zip(cs, tbs)]
            yield
            ts = [_sub_from_odd(t, _dot(_odd_blocks(t, s).astype(BF16),
                                        _block_diag(_spread_odd(m, s).astype(BF16))), s)
                  for t, m in zip(ts, ms)]
        yield
    out.extend(ts)


def _gdn_kernel(q_ref, k_ref, v_ref, z_ref, gch_ref, gct_ref, cwq_ref, cwk_ref, cwv_ref, nw_ref,
                o_ref, xpq, xpk, xpv, qs, ks, vs, kks, qks, aqs, bos, egls, osum, masks, mbf, modd,
                moddbf, selc, rowbuf, *, seq, unroll, stagger, heads, head_offset):
    c = GDN_CHUNK
    nc = seq // c
    ri = lax.broadcasted_iota(jnp.int32, (c, c), 0)
    ci = lax.broadcasted_iota(jnp.int32, (c, c), 1)
    lower, upper = ri >= ci, ri <= ci
    rp = lax.broadcasted_iota(jnp.int32, (c, 2 * c), 0)
    cp = lax.broadcasted_iota(jnp.int32, (c, 2 * c), 1) & (c - 1)
    eye_p = jnp.where(rp == cp, 1.0, 0.0)

    for lvl in range(7):
        m = jnp.where(((rp >> (lvl + 1)) == (cp >> (lvl + 1))) & ((rp >> lvl) != (cp >> lvl)), 1.0, 0.0)
        if lvl == 0:
            masks[...] = m
        elif lvl < GDN_FULL_LEVELS:
            mbf[lvl - 1] = m.astype(BF16)
        elif (1 << lvl) < GDN_BF16_ROWS:
            modd[...] = _odd_blocks(m, 1 << lvl)
        else:
            moddbf[lvl - GDN_FULL_LEVELS - 1] = _odd_blocks(m, 1 << lvl).astype(BF16)

    zeros8 = jnp.zeros((8, LANES), F32)
    rb = 2 * c
    nrb = seq // rb

    def head_program(hh):
        h = pl.program_id(1) * heads + hh
        cols = slice(hh * GDN_HEAD_DIM, (hh + 1) * GDN_HEAD_DIM)
        xq, xk, xv = xpq.at[hh], xpk.at[hh], xpv.at[hh]
        q_s, k_s, v_s, kk_s, qk_s = qs.at[hh], ks.at[hh], vs.at[hh], kks.at[hh], qks.at[hh]
        aq_s, bo_s, egl_s, o_s, sel_s, row_s = (aqs.at[hh], bos.at[hh], egls.at[hh], osum.at[hh],
                                                selc.at[hh], rowbuf.at[hh])

        l2 = lax.broadcasted_iota(jnp.int32, (2 * c, 2 * c), 0) & (c - 1)
        cb = lax.broadcasted_iota(jnp.int32, (2 * c, 2 * c), 1) >> 7
        lanes_g = (h + LANE_GF, h + LANE_GB)
        lanes_b = (h + LANE_BF, h + LANE_BB)
        for d in range(2):
            sel_s[d] = jnp.where(l2 == jnp.where(cb == 0, lanes_g[d], lanes_b[d]), 1.0, 0.0).astype(BF16)

        sub = lax.broadcasted_iota(jnp.int32, (8, seq), 0)
        for idx, lane0 in enumerate((LANE_GF, LANE_GB, LANE_BB)):
            row_s[idx:idx + 1, :] = jnp.sum(
                jnp.where(sub == h, gct_ref[lane0:lane0 + 8, :], 0.0), axis=0, keepdims=True)
        yield

        def conv_fill(x_ref, xp):
            xp[pl.ds(0, 8), :] = zeros8
            xp[pl.ds(seq + 8, 8), :] = zeros8
            xp[pl.ds(8, seq), :] = x_ref[:, cols].astype(F32)

        def conv_silu(xp, cw_ref, r):
            base = 8 - CONV_K // 2 + r * rb
            acc = cw_ref[0:1, cols] * xp[pl.ds(base, rb), :]
            for j in range(1, CONV_K):
                acc = acc + cw_ref[j:j + 1, cols] * xp[pl.ds(base + j, rb), :]
            return _silu(acc)

        def l2norm(y):
            return y * lax.rsqrt(jnp.sum(y * y, axis=-1, keepdims=True) + NORM_EPS)

        conv_fill(q_ref, xq)
        conv_fill(k_ref, xk)
        conv_fill(v_ref, xv)
        yield
        for r in range(nrb):
            q_s[r * rb:(r + 1) * rb, :] = l2norm(conv_silu(xq, cwq_ref, r)) * (GDN_HEAD_DIM ** -0.5)
            k_s[r * rb:(r + 1) * rb, :] = l2norm(conv_silu(xk, cwk_ref, r))
            for n in range(r * rb // c, (r + 1) * rb // c):
                kcb = k_s[n * c:(n + 1) * c, :].astype(BF16)
                kk_s[n] = _dot_nt(kcb, kcb)
                qk_s[n] = _dot_nt(q_s[n * c:(n + 1) * c, :].astype(BF16), kcb)
            yield
        o_s[...] = jnp.zeros((seq, LANES), F32)
        v_pending = list(range(nrb))

        def conv_v_block():
            r = v_pending.pop(0)
            v_s[r * rb:(r + 1) * rb, :] = conv_silu(xv, cwv_ref, r)

        def row_form(idx, n):
            return jnp.broadcast_to(row_s[idx:idx + 1, n * c:(n + 1) * c], (c, c))

        def prep(trip):
            pairs = [(trip * unroll + u, nc - 1 - trip * unroll - u) for u in range(unroll)]
            work, lps = [], []
            for pair in pairs:
                item = []
                for d, n in enumerate(pair):
                    sel = _dot(gch_ref[n * c:(n + 1) * c, :], sel_s[d])
                    item.append([n, sel[:, :c], sel[:, c:]])
                fw, bw = item
                n, gcol, beta = fw
                decay = jnp.exp(jnp.where(lower, gcol - row_form(0, n), MASKED_LOG))
                l_f = kk_s[n] * decay * beta
                fw.append(qk_s[n] * decay)
                n, gcol, beta = bw
                grow = row_form(1, n)
                decay = jnp.exp(jnp.where(upper, gcol - grow, MASKED_LOG))
                l_b = kk_s[n] * jnp.exp(jnp.where(lower, grow - gcol, MASKED_LOG)) * row_form(2, n)
                bw.append(qk_s[n] * decay)
                lps.append(jnp.concatenate([l_f, l_b], axis=1))
                work.append(item)
                yield
            tps = []
            yield from _unit_tri_inverse_stages(lps, masks, mbf, modd, moddbf, eye_p, tps)
            assert not v_pending
            sols = []
            for item, tp in zip(work, tps):
                for d, (n, gcol, beta, attn) in enumerate(item):
                    t_inv = tp[:, :c] if d == 0 else tp[:, c:].T
                    kc = k_s[n * c:(n + 1) * c, :]
                    eg = jnp.exp(gcol)
                    rhs = jnp.concatenate([(v_s[n * c:(n + 1) * c, :] * beta).astype(BF16),
                                           (kc * beta * eg).astype(BF16)], axis=1)
                    sols.append(_dot(t_inv.astype(BF16), rhs))
            yield
            k = 0
            for item in work:
                for d, (n, gcol, beta, attn) in enumerate(item):
                    kc = k_s[n * c:(n + 1) * c, :]
                    glast = gcol[c - 1:c, :] if d == 0 else gcol[0:1, :]
                    kdt = (kc * jnp.exp(glast - gcol)).T
                    lhs = jnp.concatenate([kdt.astype(BF16), attn.astype(BF16)], axis=0)
                    z = _dot(lhs, sols[k].astype(BF16))
                    k += 1
                    qd = q_s[n * c:(n + 1) * c, :] * jnp.exp(gcol)
                    aq_s[d, n, 0:c, :] = (-z[:c, c:]).astype(BF16)
                    aq_s[d, n, c:2 * c, :] = (qd - z[c:, c:]).astype(BF16)
                    bo_s[d, n] = z[:, :c]
                    egl_s[d, n * 8:(n + 1) * 8, :] = jnp.broadcast_to(jnp.exp(glast), (8, LANES))
            yield

        states = [jnp.zeros((GDN_HEAD_DIM, GDN_HEAD_DIM), F32)] * 2

        def scan_step(trip, u):
            for d in range(2):
                n = trip * unroll + u if d == 0 else nc - 1 - trip * unroll - u
                r = _dot(aq_s[d, n], states[d].astype(BF16)) + bo_s[d, n]
                states[d] = states[d] * egl_s[d, n * 8:n * 8 + 1, :] + r[:c]
                o_s[n * c:(n + 1) * c, :] += r[c:]

        trips = nc // unroll
        active, scan_queue, started, tick = {}, [], 0, 0
        while started < trips or active or scan_queue:
            if started < trips and tick >= started * stagger:
                active[started] = prep(started)
                started += 1
            for trip in sorted(active):
                if next(active[trip], "done") == "done":
                    del active[trip]
                    scan_queue.extend((trip, u) for u in range(unroll))
            if v_pending:
                conv_v_block()
            if scan_queue:
                scan_step(*scan_queue.pop(0))
            tick += 1
            yield

        o = o_s[...]
        y = o * lax.rsqrt(jnp.mean(o * o, axis=-1, keepdims=True) + NORM_EPS) * nw_ref[...]
        o_ref[:, cols] = (y * _silu(z_ref[:, cols].astype(F32))).astype(BF16)

    programs = [head_program(hh) for hh in range(heads)]
    live, tick = set(), 0
    while tick == 0 or live:
        for hh, prog in enumerate(programs):
            if tick == hh * head_offset:
                live.add(hh)
            if hh in live and next(prog, "done") == "done":
                live.discard(hh)
        tick += 1
    assert tick > (heads - 1) * head_offset


def _gdn(pbig, gch, gct, conv_w, norm_w, *, unroll=4, stagger=5, heads=2, head_offset=3):
    b, t, _ = pbig.shape
    c = GDN_CHUNK
    nc = t // c
    hd = GDN_HEAD_DIM
    nh = GDN_HEADS
    hw = heads * hd
    assert nc % unroll == 0 and nh % heads == 0
    nhb = nh // heads
    return pl.pallas_call(
        functools.partial(_gdn_kernel, seq=t, unroll=unroll, stagger=stagger, heads=heads,
                          head_offset=head_offset),
        grid=(b, nhb),
        in_specs=[
            pl.BlockSpec((None, t, hw), lambda i, j: (i, 0, j)),
            pl.BlockSpec((None, t, hw), lambda i, j: (i, 0, nhb + j)),
            pl.BlockSpec((None, t, hw), lambda i, j: (i, 0, 2 * nhb + j)),
            pl.BlockSpec((None, t, hw), lambda i, j: (i, 0, COL_Z // hw + j)),
            pl.BlockSpec((None, t, 2 * LANES), lambda i, j: (i, 0, 0)),
            pl.BlockSpec((None, LANES, t), lambda i, j: (i, 0, 0)),
            pl.BlockSpec((CONV_K, hw), lambda i, j: (0, j)),
            pl.BlockSpec((CONV_K, hw), lambda i, j: (0, nhb + j)),
            pl.BlockSpec((CONV_K, hw), lambda i, j: (0, 2 * nhb + j)),
            pl.BlockSpec((1, hd), lambda i, j: (0, 0)),
        ],
        out_specs=pl.BlockSpec((None, t, hw), lambda i, j: (i, 0, j)),
        out_shape=jax.ShapeDtypeStruct((b, t, GDN_WIDTH), BF16),
        scratch_shapes=[
            pltpu.VMEM((heads, t + 16, hd), F32),
            pltpu.VMEM((heads, t + 16, hd), F32),
            pltpu.VMEM((heads, t + 16, hd), F32),
            pltpu.VMEM((heads, t, hd), F32),
            pltpu.VMEM((heads, t, hd), F32),
            pltpu.VMEM((heads, t, hd), F32),
            pltpu.VMEM((heads, nc, c, c), F32),
            pltpu.VMEM((heads, nc, c, c), F32),
            pltpu.VMEM((heads, 2, nc, 2 * c, hd), BF16),
            pltpu.VMEM((heads, 2, nc, 2 * c, hd), F32),
            pltpu.VMEM((heads, 2, nc * 8, LANES), F32),
            pltpu.VMEM((heads, t, hd), F32),
            pltpu.VMEM((c, 2 * c), F32),
            pltpu.VMEM((GDN_FULL_LEVELS - 1, c, 2 * c), BF16),
            pltpu.VMEM((c // 2, 2 * c), F32),
            pltpu.VMEM((3, c // 2, 2 * c), BF16),
            pltpu.VMEM((heads, 2, 2 * c, 2 * c), BF16),
            pltpu.VMEM((heads, 8, t), F32),
        ],
        compiler_params=pltpu.CompilerParams(
            dimension_semantics=("arbitrary", "arbitrary"), vmem_limit_bytes=VMEM_LIMIT),
        name="gdn",
    )(pbig, pbig, pbig, pbig, gch, gct, conv_w, conv_w, conv_w, norm_w)


def _gla_kernel(q_ref, k_ref, v_ref, g_ref, gc_ref, w2f_ref, w2b_ref, b2f_ref, b2b_ref, nw_ref,
                o_ref, tris, sst, osum, *, seq, unroll, heads):
    c2 = 2 * GLA_CHUNK
    npair = seq // c2
    dk = GLA_HEAD_K
    dv = GLA_HEAD_V
    ri = lax.broadcasted_iota(jnp.int32, (c2, c2), 0)
    ci = lax.broadcasted_iota(jnp.int32, (c2, c2), 1)
    same = (ri >> 6) == (ci >> 6)
    incls = [same & (ri >= ci), same & (ri <= ci)]
    crosses = [(ri >= GLA_CHUNK) & (ci < GLA_CHUNK), (ri < GLA_CHUNK) & (ci >= GLA_CHUNK)]
    row_k = lax.broadcasted_iota(jnp.int32, (c2, dk), 0)
    scanned_first = [row_k < GLA_CHUNK, row_k >= GLA_CHUNK]
    total_rows = [(GLA_CHUNK - 1, c2 - 1), (GLA_CHUNK, 0)]
    w2_refs = (w2f_ref, w2b_ref)
    b2_refs = (b2f_ref, b2b_ref)

    for d in range(2):
        tris[d] = jnp.where(incls[d], 1.0, 0.0).astype(BF16)
    osum[...] = jnp.zeros((seq, heads * dv), F32)
    sst[...] = jnp.zeros((heads * 2, dk, dv), F32)

    def pairs(it, _):
        items = []
        for u in range(unroll):
            for hh in range(heads):
                for d in range(2):
                    pp = it * unroll + u
                    pp = pp if d == 0 else npair - 1 - pp
                    rows = pl.ds(pl.multiple_of(pp * c2, c2), c2)
                    kc = slice(hh * dk, (hh + 1) * dk)
                    pre = _dot(gc_ref[rows, :], w2_refs[d][:, kc]) + b2_refs[d][:, kc]
                    items.append([(hh, d), rows, pre])
        for x in items:
            gk = _log_sigmoid(x[2]) / GLA_GATE_NORMALIZER
            cs = _dot(tris[x[0][1]], jnp.concatenate(_split2(gk), axis=1))
            x[2] = cs[:, :dk] + cs[:, dk:]
        for x in items:
            (hh, d), rows, gcum = x
            kc = slice(hh * dk, (hh + 1) * dk)
            r1, r2 = total_rows[d]
            tot1, tot2 = gcum[r1:r1 + 1, :], gcum[r2:r2 + 1, :]
            q = q_ref[rows, kc].astype(F32) * (dk ** -0.5)
            k = k_ref[rows, kc].astype(F32)
            qg = q * jnp.exp(gcum)
            qgb = qg.astype(BF16)
            kg = (k * jnp.exp(-gcum)).astype(BF16)
            kd = k * jnp.exp(jnp.where(scanned_first[d], tot1, tot2) - gcum)
            attn = jnp.where(incls[d], _dot_nt(qgb, kg),
                             jnp.where(crosses[d], _dot_nt(qgb, kd.astype(BF16)), 0.0))
            q_in = jnp.where(scanned_first[d], qg, qg * jnp.exp(tot1)).astype(BF16)
            kd_out = jnp.where(scanned_first[d], kd * jnp.exp(tot2), kd)
            gt = gcum.T
            e_col = jnp.exp(gt[:, r1:r1 + 1] + gt[:, r2:r2 + 1])
            x[2:] = [q_in, attn, kd_out.T.astype(BF16), e_col]
        for x in items:
            (hh, d), rows, q_in, attn, kdt, e_col = x
            vb = v_ref[rows, hh * dv:(hh + 1) * dv]
            x[3] = _dot(attn.astype(BF16), vb)
            x[4] = _dot(kdt, vb)
        states = [sst[i] for i in range(heads * 2)]
        for x in items:
            (hh, d), rows, q_in, intra, upd, e_col = x
            osum[rows, hh * dv:(hh + 1) * dv] += _dot(q_in, states[hh * 2 + d].astype(BF16)) + intra
            states[hh * 2 + d] = states[hh * 2 + d] * e_col + upd
        for i in range(heads * 2):
            sst[i] = states[i]
        return 0

    lax.fori_loop(0, npair // unroll, pairs, 0)

    for hh in range(heads):
        vc = slice(hh * dv, (hh + 1) * dv)
        o = osum[:, vc]
        y = o * lax.rsqrt(jnp.mean(o * o, axis=-1, keepdims=True) + NORM_EPS) * nw_ref[...]
        o_ref[:, vc] = (y * _silu(g_ref[:, vc].astype(F32))).astype(BF16)


def _gla(pbig, gc, w2f, w2b, b2f, b2b, norm_w, *, unroll=4, heads=2):
    b, t, _ = pbig.shape
    dk = GLA_HEAD_K
    dv = GLA_HEAD_V
    kw, vw = heads * dk, heads * dv
    assert GLA_HEADS % heads == 0
    return pl.pallas_call(
        functools.partial(_gla_kernel, seq=t, unroll=unroll, heads=heads),
        grid=(b, GLA_HEADS // heads),
        in_specs=[
            pl.BlockSpec((None, t, kw), lambda i, j: (i, 0, COL_QB // kw + j)),
            pl.BlockSpec((None, t, kw), lambda i, j: (i, 0, COL_KB // kw + j)),
            pl.BlockSpec((None, t, vw), lambda i, j: (i, 0, COL_VB // vw + j)),
            pl.BlockSpec((None, t, vw), lambda i, j: (i, 0, COL_GB // vw + j)),
            pl.BlockSpec((None, t, LANES), lambda i, j: (i, 0, 0)),
            pl.BlockSpec((LANES, kw), lambda i, j: (0, j)),
            pl.BlockSpec((LANES, kw), lambda i, j: (0, j)),
            pl.BlockSpec((1, kw), lambda i, j: (0, j)),
            pl.BlockSpec((1, kw), lambda i, j: (0, j)),
            pl.BlockSpec((1, dv), lambda i, j: (0, 0)),
        ],
        out_specs=pl.BlockSpec((None, t, vw), lambda i, j: (i, 0, j)),
        out_shape=jax.ShapeDtypeStruct((b, t, GLA_VAL_DIM), BF16),
        scratch_shapes=[
            pltpu.VMEM((2, 2 * GLA_CHUNK, 2 * GLA_CHUNK), BF16),
            pltpu.VMEM((heads * 2, dk, dv), F32),
            pltpu.VMEM((t, vw), F32),
        ],
        compiler_params=pltpu.CompilerParams(
            dimension_semantics=("arbitrary", "arbitrary"), vmem_limit_bytes=VMEM_LIMIT),
        name="gla",
    )(pbig, pbig, pbig, pbig, gc, w2f, w2b, b2f, b2b, norm_w)


def _merge_kernel(oa_ref, ob_ref, ga_ref, gb_ref, x_ref, wa_ref, wb_ref, wo_ref, lnw_ref, out_ref, *,
                  row_blocks):
    rows = out_ref.shape[0] // row_blocks
    for r in range(row_blocks):
        rs = slice(r * rows, (r + 1) * rows)
        ya = _dot(oa_ref[rs, :], wa_ref[...])
        yb = _dot(ob_ref[rs, :], wb_ref[...])
        merged = (_sigmoid(ga_ref[rs, :].astype(F32)) * ya + _sigmoid(gb_ref[rs, :].astype(F32)) * yb)
        out = _dot(merged.astype(BF16), wo_ref[...])
        y = out * lax.rsqrt(jnp.mean(out * out, axis=-1, keepdims=True) + NORM_EPS) * lnw_ref[...]
        out_ref[rs, :] = x_ref[rs, :] + y


def _merge(oa, ob, pbig2d, x2d, wa, wb, wo, ln_w, *, tm=1024, row_blocks=4):
    m, d = x2d.shape
    return pl.pallas_call(
        functools.partial(_merge_kernel, row_blocks=row_blocks),
        grid=(m // tm,),
        in_specs=[
            pl.BlockSpec((tm, d), lambda i: (i, 0)),
            pl.BlockSpec((tm, d), lambda i: (i, 0)),
            pl.BlockSpec((tm, d), lambda i: (i, COL_GATE_A // D_MODEL)),
            pl.BlockSpec((tm, d), lambda i: (i, COL_GATE_B // D_MODEL)),
            pl.BlockSpec((tm, d), lambda i: (i, 0)),
            pl.BlockSpec((d, d), lambda i: (0, 0)),
            pl.BlockSpec((d, d), lambda i: (0, 0)),
            pl.BlockSpec((d, d), lambda i: (0, 0)),
            pl.BlockSpec((1, d), lambda i: (0, 0)),
        ],
        out_specs=pl.BlockSpec((tm, d), lambda i: (i, 0)),
        out_shape=jax.ShapeDtypeStruct((m, d), F32),
        compiler_params=pltpu.CompilerParams(
            dimension_semantics=("arbitrary",), vmem_limit_bytes=VMEM_LIMIT),
        name="merge",
    )(oa, ob, pbig2d, pbig2d, x2d, wa, wb, wo, ln_w)


def _layer(x, ln_pre_w, w_in, conv_w, a_log_fwd, a_log_bwd, dt_bias_fwd, dt_bias_bwd,
           gdn_norm_w, w_proj_gdn, gk_w2_fwd, gk_b2_fwd, gk_w2_bwd, gk_b2_bwd,
           gla_norm_w, w_proj_gla, w_out, ln_post_w):
    b, t, d = x.shape
    m = b * t
    w_big, w_small = _wprep(w_in.T)
    pad = jnp.zeros((LANES - 2 * GDN_HEADS,), F32)
    alog_vec = jnp.concatenate([a_log_fwd, a_log_bwd, pad]).reshape(1, LANES)
    dtb_vec = jnp.concatenate([dt_bias_fwd, dt_bias_bwd, pad]).reshape(1, LANES)

    def w2_ext(w2, lane0):
        z = jnp.zeros((LANES, GLA_KEY_DIM), F32)
        return z.at[lane0:lane0 + GLA_GATE_RANK, :].set(w2).astype(BF16)

    x2d = x.reshape(m, d)
    pbig, psmall = _inproj(x2d, ln_pre_w.reshape(1, d), w_big, w_small)
    gch, gct = _gates(psmall.reshape(b, t, LANES), alog_vec, dtb_vec)
    pbig3 = pbig.reshape(b, t, N_BIG)
    oa = _gdn(pbig3, gch, gct, conv_w, gdn_norm_w.reshape(1, GDN_HEAD_DIM))
    ob = _gla(pbig3, gch, w2_ext(gk_w2_fwd, LANE_RF), w2_ext(gk_w2_bwd, LANE_RB),
              gk_b2_fwd.reshape(1, GLA_KEY_DIM), gk_b2_bwd.reshape(1, GLA_KEY_DIM),
              gla_norm_w.reshape(1, GLA_HEAD_V))
    out = _merge(oa.reshape(m, GDN_WIDTH), ob.reshape(m, GLA_VAL_DIM), pbig, x2d,
                 w_proj_gdn.astype(BF16), w_proj_gla.astype(BF16), w_out.astype(BF16),
                 ln_post_w.reshape(1, d))
    return out.reshape(b, t, d)


def kernel(x, ln_pre_w, w_in, conv_w, a_log_fwd, a_log_bwd, dt_bias_fwd, dt_bias_bwd, gdn_norm_w,
           w_proj_gdn, gk_w2_fwd, gk_b2_fwd, gk_w2_bwd, gk_b2_bwd, gla_norm_w, w_proj_gla, w_out,
           ln_post_w):
    h = x
    for l in range(ln_pre_w.shape[0]):
        h = _layer(h, ln_pre_w[l], w_in[l], conv_w[l], a_log_fwd[l], a_log_bwd[l],
                   dt_bias_fwd[l], dt_bias_bwd[l], gdn_norm_w[l], w_proj_gdn[l],
                   gk_w2_fwd[l], gk_b2_fwd[l], gk_w2_bwd[l], gk_b2_bwd[l],
                   gla_norm_w[l], w_proj_gla[l], w_out[l], ln_post_w[l])
    return h
```

```python
import functools

import jax
import jax.numpy as jnp
from jax import lax
from jax.experimental import pallas as pl
from jax.experimental.pallas import tpu as pltpu

F32 = jnp.float32
BF16 = jnp.bfloat16

D_MODEL = 1024
GDN_HEADS = 8
GDN_HEAD_DIM = 128
GDN_WIDTH = GDN_HEADS * GDN_HEAD_DIM
CONV_K = 5
GLA_HEADS = 4
GLA_KEY_DIM = D_MODEL // 2
GLA_VAL_DIM = D_MODEL
GLA_HEAD_K = GLA_KEY_DIM // GLA_HEADS
GLA_HEAD_V = GLA_VAL_DIM // GLA_HEADS
GLA_GATE_RANK = 16
GLA_GATE_NORMALIZER = 16.0
GLA_CHUNK = 64
NORM_EPS = 1e-6
MASKED_LOG = -1e30

LANES = 128
GDN_CHUNK = LANES
VMEM_LIMIT = 56 * 1024 * 1024

_IN_SIZES = [3 * GDN_WIDTH, GDN_WIDTH, GDN_HEADS, GDN_HEADS, GDN_HEADS, GDN_HEADS,
             GLA_KEY_DIM, GLA_KEY_DIM, GLA_VAL_DIM, GLA_VAL_DIM,
             GLA_GATE_RANK, GLA_GATE_RANK, D_MODEL, D_MODEL]
_OFF = [0]
for _s in _IN_SIZES:
    _OFF.append(_OFF[-1] + _s)
N_BIG = 3 * GDN_WIDTH + GDN_WIDTH + 2 * GLA_KEY_DIM + 2 * GLA_VAL_DIM + 2 * D_MODEL
COL_Z = 3 * GDN_WIDTH
COL_QB = COL_Z + GDN_WIDTH
COL_KB = COL_QB + GLA_KEY_DIM
COL_VB = COL_KB + GLA_KEY_DIM
COL_GB = COL_VB + GLA_VAL_DIM
COL_GATE_A = COL_GB + GLA_VAL_DIM
COL_GATE_B = COL_GATE_A + D_MODEL
LANE_GF, LANE_GB, LANE_BF, LANE_BB = 0, 8, 16, 24
LANE_RF, LANE_RB = 32, 48


def _dot(a, b):
    return jnp.dot(a, b, preferred_element_type=F32)


def _dot_nt(a, b):
    return lax.dot_general(a, b, (((1,), (1,)), ((), ())), preferred_element_type=F32)


def _split2(x):
    hi = x.astype(BF16)
    lo = (x - hi.astype(F32)).astype(BF16)
    return hi, lo


def _split3(x):
    hi = x.astype(BF16)
    r = x - hi.astype(F32)
    mid = r.astype(BF16)
    lo = (r - mid.astype(F32)).astype(BF16)
    return hi, mid, lo


def _sigmoid(x):
    return jax.nn.sigmoid(x)


def _silu(x):
    return x * _sigmoid(x)


def _softplus(x):
    return jnp.maximum(x, 0.0) + jnp.log(1.0 + jnp.exp(-jnp.abs(x)))


def _log_sigmoid(x):
    return jnp.minimum(x, 0.0) - jnp.log(1.0 + jnp.exp(-jnp.abs(x)))


def _wide_tile_row(j, tn):
    unit = _OFF[6] - _OFF[2]
    assert _OFF[12] - _OFF[10] == unit and _OFF[2] % unit == 0 and _OFF[10] % unit == 0 and tn % unit == 0
    start = j * (tn // unit)
    for first_wide_col in (_OFF[2], _OFF[10] - unit):
        start = start + jnp.where(j * tn >= first_wide_col, 1, 0)
    return start * unit


def _wprep_kernel(w_ref, na_ref, nb_ref, o_ref, osmall_ref):
    o_ref[...] = w_ref[...].astype(BF16)

    @pl.when(pl.program_id(0) == 0)
    def _():
        unit = na_ref.shape[0]
        osmall_ref[0:unit, :] = na_ref[...].astype(BF16)
        osmall_ref[unit:2 * unit, :] = nb_ref[...].astype(BF16)
        osmall_ref[2 * unit:, :] = jnp.zeros((LANES - 2 * unit, osmall_ref.shape[1]), BF16)


def _wprep(w_t, *, tn=1024):
    n, d = w_t.shape
    unit = _OFF[6] - _OFF[2]
    assert all(seg % tn == 0 for seg in (_OFF[2], _OFF[10] - _OFF[6], _OFF[14] - _OFF[12]))
    return pl.pallas_call(
        _wprep_kernel,
        grid=(N_BIG // tn,),
        in_specs=[
            pl.BlockSpec((pl.Element(tn), pl.Element(d)), lambda j: (_wide_tile_row(j, tn), 0)),
            pl.BlockSpec((unit, d), lambda j: (_OFF[2] // unit, 0)),
            pl.BlockSpec((unit, d), lambda j: (_OFF[10] // unit, 0)),
        ],
        out_specs=[
            pl.BlockSpec((tn, d), lambda j: (j, 0)),
            pl.BlockSpec((LANES, d), lambda j: (0, 0)),
        ],
        out_shape=[
            jax.ShapeDtypeStruct((N_BIG, d), BF16),
            jax.ShapeDtypeStruct((LANES, d), BF16),
        ],
        compiler_params=pltpu.CompilerParams(
            dimension_semantics=("arbitrary",), vmem_limit_bytes=VMEM_LIMIT),
        name="wprep",
    )(w_t, w_t, w_t)


def _inproj_kernel(x_ref, lnw_ref, wbig_ref, wsmall_ref, pbig_ref, psmall_ref, h_ref, *, tm):
    j = pl.program_id(1)

    @pl.when(j == 0)
    def _():
        def body(r, _):
            rows = pl.ds(pl.multiple_of(r * LANES, LANES), LANES)
            x = x_ref[rows, :]
            ms = jnp.mean(x * x, axis=-1, keepdims=True)
            h_ref[rows, :] = (x * lax.rsqrt(ms + NORM_EPS) * lnw_ref[...]).astype(BF16)
            return 0
        lax.fori_loop(0, tm // LANES, body, 0)
        psmall_ref[...] = _dot_nt(h_ref[...], wsmall_ref[...])

    pbig_ref[...] = _dot_nt(h_ref[...], wbig_ref[...]).astype(BF16)


def _inproj(x2d, ln_w, w_big, w_small, *, tm=1024, tn=3072):
    m, d = x2d.shape
    return pl.pallas_call(
        functools.partial(_inproj_kernel, tm=tm),
        grid=(m // tm, N_BIG // tn),
        in_specs=[
            pl.BlockSpec((tm, d), lambda i, j: (i, 0)),
            pl.BlockSpec((1, d), lambda i, j: (0, 0)),
            pl.BlockSpec((tn, d), lambda i, j: (j, 0)),
            pl.BlockSpec((LANES, d), lambda i, j: (0, 0)),
        ],
        out_specs=[
            pl.BlockSpec((tm, tn), lambda i, j: (i, j)),
            pl.BlockSpec((tm, LANES), lambda i, j: (i, 0)),
        ],
        out_shape=[
            jax.ShapeDtypeStruct((m, N_BIG), BF16),
            jax.ShapeDtypeStruct((m, LANES), F32),
        ],
        scratch_shapes=[pltpu.VMEM((tm, d), BF16)],
        compiler_params=pltpu.CompilerParams(
            dimension_semantics=("arbitrary", "arbitrary"),
            vmem_limit_bytes=VMEM_LIMIT),
        name="inproj",
    )(x2d, ln_w, w_big, w_small)


def _gates_kernel(ps_ref, alog_ref, dtb_ref, out_ref, out_t_ref, *, seq):
    c = GDN_CHUNK
    ri = lax.broadcasted_iota(jnp.int32, (c, c), 0)
    ci = lax.broadcasted_iota(jnp.int32, (c, c), 1)
    tri_l = jnp.where(ri >= ci, 1.0, 0.0).astype(BF16)
    tri_u = jnp.where(ri <= ci, 1.0, 0.0).astype(BF16)
    lane = lax.broadcasted_iota(jnp.int32, (c, LANES), 1)
    neg_a = -jnp.exp(alog_ref[...])

    for n in range(seq // c):
        a = ps_ref[n * c:(n + 1) * c, :]
        g = neg_a * _softplus(a + dtb_ref[...])
        beta = _sigmoid(a)
        parts = jnp.concatenate(_split3(g), axis=1)
        cl = _dot(tri_l, parts)
        cu = _dot(tri_u, parts)
        cl = cl[:, :LANES] + cl[:, LANES:2 * LANES] + cl[:, 2 * LANES:]
        cu = cu[:, :LANES] + cu[:, LANES:2 * LANES] + cu[:, 2 * LANES:]
        res = jnp.where(
            lane < LANE_GB, cl, jnp.where(lane < LANE_BF, cu, jnp.where(lane < LANE_RF, beta, a)))
        out_ref[n * c:(n + 1) * c, :] = jnp.concatenate(_split2(res), axis=1)
        out_t_ref[:, n * c:(n + 1) * c] = res.T


def _gates(ps, alog_vec, dtb_vec):
    b, t, _ = ps.shape
    return pl.pallas_call(
        functools.partial(_gates_kernel, seq=t),
        grid=(b,),
        in_specs=[
            pl.BlockSpec((None, t, LANES), lambda i: (i, 0, 0)),
            pl.BlockSpec((1, LANES), lambda i: (0, 0)),
            pl.BlockSpec((1, LANES), lambda i: (0, 0)),
        ],
        out_specs=[
            pl.BlockSpec((None, t, 2 * LANES), lambda i: (i, 0, 0)),
            pl.BlockSpec((None, LANES, t), lambda i: (i, 0, 0)),
        ],
        out_shape=[
            jax.ShapeDtypeStruct((b, t, 2 * LANES), BF16),
            jax.ShapeDtypeStruct((b, LANES, t), F32),
        ],
        compiler_params=pltpu.CompilerParams(
            dimension_semantics=("arbitrary",), vmem_limit_bytes=VMEM_LIMIT),
        name="gates",
    )(ps, alog_vec, dtb_vec)


def _block_diag(xp):
    c = xp.shape[0]
    z = jnp.zeros((c, c), xp.dtype)
    return jnp.concatenate([jnp.concatenate([xp[:, :c], z], axis=1),
                            jnp.concatenate([z, xp[:, c:]], axis=1)], axis=0)


def _odd_blocks(x, s):
    return jnp.concatenate([x[b * s:(b + 1) * s] for b in range(1, x.shape[0] // s, 2)], axis=0)


def _spread_odd(x_odd, s):
    z = jnp.zeros((s, x_odd.shape[1]), x_odd.dtype)
    parts = []
    for j in range(x_odd.shape[0] // s):
        parts += [z, x_odd[j * s:(j + 1) * s]]
    return jnp.concatenate(parts, axis=0)


def _sub_from_odd(x, r_odd, s):
    parts = []
    for b in range(x.shape[0] // s):
        blk = x[b * s:(b + 1) * s]
        if b % 2:
            blk = blk - r_odd[(b // 2) * s:(b // 2 + 1) * s]
        parts.append(blk)
    return jnp.concatenate(parts, axis=0)


GDN_FULL_LEVELS = 3
GDN_BF16_ROWS = 16


def _unit_tri_inverse_stages(lps, masks_ref, mbf_ref, modd_ref, moddbf_ref, eye_p, out):
    ts = [eye_p - lp * masks_ref[...] for lp in lps]
    lbs = [lp.astype(BF16) for lp in lps]
    for lvl in range(1, 7):
        s = 1 << lvl
        tbs = [t.astype(BF16) for t in ts]
        if lvl < GDN_FULL_LEVELS:
            ms = [_dot(lb * mbf_ref[lvl - 1], _block_diag(tb)) for lb, tb in zip(lbs, tbs)]
            yield
            ts = [t - _dot(tb, _block_diag(m.astype(BF16))) for t, tb, m in zip(ts, tbs, ms)]
        else:
            if s < GDN_BF16_ROWS:
                cs = [(_odd_blocks(lp, s) * modd_ref[...]).astype(BF16) for lp in lps]
            else:
                mo = moddbf_ref[lvl - GDN_FULL_LEVELS - 1]
                cs = [_odd_blocks(lb, s) * mo for lb in lbs]
            ms = [_dot(cm, _block_diag(tb)) for cm, tb in zip(cs, tbs)]
            yield
            ts = [_sub_from_odd(t, _dot(_odd_blocks(t, s).astype(BF16),
                                        _block_diag(_spread_odd(m, s).astype(BF16))), s)
                  for t, m in zip(ts, ms)]
        yield
    out.extend(ts)


def _gdn_kernel(q_ref, k_ref, v_ref, z_ref, gch_ref, gct_ref, cwq_ref, cwk_ref, cwv_ref, nw_ref,
                o_ref, xpq, xpk, xpv, qs, ks, vs, kks, qks, aqs, bos, egls, osum, masks, mbf, modd,
                moddbf, selc, rowbuf, *, seq, unroll, stagger, heads, head_offset):
    c = GDN_CHUNK
    nc = seq // c
    ri = lax.broadcasted_iota(jnp.int32, (c, c), 0)
    ci = lax.broadcasted_iota(jnp.int32, (c, c), 1)
    lower, upper = ri >= ci, ri <= ci
    rp = lax.broadcasted_iota(jnp.int32, (c, 2 * c), 0)
    cp = lax.broadcasted_iota(jnp.int32, (c, 2 * c), 1) & (c - 1)
    eye_p = jnp.where(rp == cp, 1.0, 0.0)

    for lvl in range(7):
        m = jnp.where(((rp >> (lvl + 1)) == (cp >> (lvl + 1))) & ((rp >> lvl) != (cp >> lvl)), 1.0, 0.0)
        if lvl == 0:
            masks[...] = m
        elif lvl < GDN_FULL_LEVELS:
            mbf[lvl - 1] = m.astype(BF16)
        elif (1 << lvl) < GDN_BF16_ROWS:
            modd[...] = _odd_blocks(m, 1 << lvl)
        else:
            moddbf[lvl - GDN_FULL_LEVELS - 1] = _odd_blocks(m, 1 << lvl).astype(BF16)

    zeros8 = jnp.zeros((8, LANES), F32)
    rb = 2 * c
    nrb = seq // rb

    def head_program(hh):
        h = pl.program_id(1) * heads + hh
        cols = slice(hh * GDN_HEAD_DIM, (hh + 1) * GDN_HEAD_DIM)
        xq, xk, xv = xpq.at[hh], xpk.at[hh], xpv.at[hh]
        q_s, k_s, v_s, kk_s, qk_s = qs.at[hh], ks.at[hh], vs.at[hh], kks.at[hh], qks.at[hh]
        aq_s, bo_s, egl_s, o_s, sel_s, row_s = (aqs.at[hh], bos.at[hh], egls.at[hh], osum.at[hh],
                                                selc.at[hh], rowbuf.at[hh])

        l2 = lax.broadcasted_iota(jnp.int32, (2 * c, 2 * c), 0) & (c - 1)
        cb = lax.broadcasted_iota(jnp.int32, (2 * c, 2 * c), 1) >> 7
        lanes_g = (h + LANE_GF, h + LANE_GB)
        lanes_b = (h + LANE_BF, h + LANE_BB)
        for d in range(2):
            sel_s[d] = jnp.where(l2 == jnp.where(cb == 0, lanes_g[d], lanes_b[d]), 1.0, 0.0).astype(BF16)

        sub = lax.broadcasted_iota(jnp.int32, (8, seq), 0)
        for idx, lane0 in enumerate((LANE_GF, LANE_GB, LANE_BB)):
            row_s[idx:idx + 1, :] = jnp.sum(
                jnp.where(sub == h, gct_ref[lane0:lane0 + 8, :], 0.0), axis=0, keepdims=True)
        yield

        def conv_fill(x_ref, xp):
            xp[pl.ds(0, 8), :] = zeros8
            xp[pl.ds(seq + 8, 8), :] = zeros8
            xp[pl.ds(8, seq), :] = x_ref[:, cols].astype(F32)

        def conv_silu(xp, cw_ref, r):
            base = 8 - CONV_K // 2 + r * rb
            acc = cw_ref[0:1, cols] * xp[pl.ds(base, rb), :]
            for j in range(1, CONV_K):
                acc = acc + cw_ref[j:j + 1, cols] * xp[pl.ds(base + j, rb), :]
            return _silu(acc)

        def l2norm(y):
            return y * lax.rsqrt(jnp.sum(y * y, axis=-1, keepdims=True) + NORM_EPS)

        conv_fill(q_ref, xq)
        conv_fill(k_ref, xk)
        conv_fill(v_ref, xv)
        yield
        for r in range(nrb):
            q_s[r * rb:(r + 1) * rb, :] = l2norm(conv_silu(xq, cwq_ref, r)) * (GDN_HEAD_DIM ** -0.5)
            k_s[r * rb:(r + 1) * rb, :] = l2norm(conv_silu(xk, cwk_ref, r))
            for n in range(r * rb // c, (r + 1) * rb // c):
                kcb = k_s[n * c:(n + 1) * c, :].astype(BF16)
                kk_s[n] = _dot_nt(kcb, kcb)
                qk_s[n] = _dot_nt(q_s[n * c:(n + 1) * c, :].astype(BF16), kcb)
            yield
        o_s[...] = jnp.zeros((seq, LANES), F32)
        v_pending = list(range(nrb))

        def conv_v_block():
            r = v_pending.pop(0)
            v_s[r * rb:(r + 1) * rb, :] = conv_silu(xv, cwv_ref, r)

        def row_form(idx, n):
            return jnp.broadcast_to(row_s[idx:idx + 1, n * c:(n + 1) * c], (c, c))

        def prep(trip):
            pairs = [(trip * unroll + u, nc - 1 - trip * unroll - u) for u in range(unroll)]
            work, lps = [], []
            for pair in pairs:
                item = []
                for d, n in enumerate(pair):
                    sel = _dot(gch_ref[n * c:(n + 1) * c, :], sel_s[d])
                    item.append([n, sel[:, :c], sel[:, c:]])
                fw, bw = item
                n, gcol, beta = fw
                decay = jnp.exp(jnp.where(lower, gcol - row_form(0, n), MASKED_LOG))
                l_f = kk_s[n] * decay * beta
                fw.append(qk_s[n] * decay)
                n, gcol, beta = bw
                grow = row_form(1, n)
                decay = jnp.exp(jnp.where(upper, gcol - grow, MASKED_LOG))
                l_b = kk_s[n] * jnp.exp(jnp.where(lower, grow - gcol, MASKED_LOG)) * row_form(2, n)
                bw.append(qk_s[n] * decay)
                lps.append(jnp.concatenate([l_f, l_b], axis=1))
                work.append(item)
                yield
            tps = []
            yield from _unit_tri_inverse_stages(lps, masks, mbf, modd, moddbf, eye_p, tps)
            assert not v_pending
            sols = []
            for item, tp in zip(work, tps):
                for d, (n, gcol, beta, attn) in enumerate(item):
                    t_inv = tp[:, :c] if d == 0 else tp[:, c:].T
                    kc = k_s[n * c:(n + 1) * c, :]
                    eg = jnp.exp(gcol)
                    rhs = jnp.concatenate([(v_s[n * c:(n + 1) * c, :] * beta).astype(BF16),
                                           (kc * beta * eg).astype(BF16)], axis=1)
                    sols.append(_dot(t_inv.astype(BF16), rhs))
            yield
            k = 0
            for item in work:
                for d, (n, gcol, beta, attn) in enumerate(item):
                    kc = k_s[n * c:(n + 1) * c, :]
                    glast = gcol[c - 1:c, :] if d == 0 else gcol[0:1, :]
                    kdt = (kc * jnp.exp(glast - gcol)).T
                    lhs = jnp.concatenate([kdt.astype(BF16), attn.astype(BF16)], axis=0)
                    z = _dot(lhs, sols[k].astype(BF16))
                    k += 1
                    qd = q_s[n * c:(n + 1) * c, :] * jnp.exp(gcol)
                    aq_s[d, n, 0:c, :] = (-z[:c, c:]).astype(BF16)
                    aq_s[d, n, c:2 * c, :] = (qd - z[c:, c:]).astype(BF16)
                    bo_s[d, n] = z[:, :c]
                    egl_s[d, n * 8:(n + 1) * 8, :] = jnp.broadcast_to(jnp.exp(glast), (8, LANES))
            yield

        states = [jnp.zeros((GDN_HEAD_DIM, GDN_HEAD_DIM), F32)] * 2

        def scan_step(trip, u):
            for d in range(2):
                n = trip * unroll + u if d == 0 else nc - 1 - trip * unroll - u
                r = _dot(aq_s[d, n], states[d].astype(BF16)) + bo_s[d, n]
                states[d] = states[d] * egl_s[d, n * 8:n * 8 + 1, :] + r[:c]
                o_s[n * c:(n + 1) * c, :] += r[c:]

        trips = nc // unroll
        active, scan_queue, started, tick = {}, [], 0, 0
        while started < trips or active or scan_queue:
            if started < trips and tick >= started * stagger:
                active[started] = prep(started)
                started += 1
            for trip in sorted(active):
                if next(active[trip], "done") == "done":
                    del active[trip]
                    scan_queue.extend((trip, u) for u in range(unroll))
            if v_pending:
                conv_v_block()
            if scan_queue:
                scan_step(*scan_queue.pop(0))
            tick += 1
            yield

        o = o_s[...]
        y = o * lax.rsqrt(jnp.mean(o * o, axis=-1, keepdims=True) + NORM_EPS) * nw_ref[...]
        o_ref[:, cols] = (y * _silu(z_ref[:, cols].astype(F32))).astype(BF16)

    programs = [head_program(hh) for hh in range(heads)]
    live, tick = set(), 0
    while tick == 0 or live:
        for hh, prog in enumerate(programs):
            if tick == hh * head_offset:
                live.add(hh)
            if hh in live and next(prog, "done") == "done":
                live.discard(hh)
        tick += 1
    assert tick > (heads - 1) * head_offset


def _gdn(pbig, gch, gct, conv_w, norm_w, *, unroll=4, stagger=5, heads=2, head_offset=3):
    b, t, _ = pbig.shape
    c = GDN_CHUNK
    nc = t // c
    hd = GDN_HEAD_DIM
    nh = GDN_HEADS
    hw = heads * hd
    assert nc % unroll == 0 and nh % heads == 0
    nhb = nh // heads
    return pl.pallas_call(
        functools.partial(_gdn_kernel, seq=t, unroll=unroll, stagger=stagger, heads=heads,
                          head_offset=head_offset),
        grid=(b, nhb),
        in_specs=[
            pl.BlockSpec((None, t, hw), lambda i, j: (i, 0, j)),
            pl.BlockSpec((None, t, hw), lambda i, j: (i, 0, nhb + j)),
            pl.BlockSpec((None, t, hw), lambda i, j: (i, 0, 2 * nhb + j)),
            pl.BlockSpec((None, t, hw), lambda i, j: (i, 0, COL_Z // hw + j)),
            pl.BlockSpec((None, t, 2 * LANES), lambda i, j: (i, 0, 0)),
            pl.BlockSpec((None, LANES, t), lambda i, j: (i, 0, 0)),
            pl.BlockSpec((CONV_K, hw), lambda i, j: (0, j)),
            pl.BlockSpec((CONV_K, hw), lambda i, j: (0, nhb + j)),
            pl.BlockSpec((CONV_K, hw), lambda i, j: (0, 2 * nhb + j)),
            pl.BlockSpec((1, hd), lambda i, j: (0, 0)),
        ],
        out_specs=pl.BlockSpec((None, t, hw), lambda i, j: (i, 0, j)),
        out_shape=jax.ShapeDtypeStruct((b, t, GDN_WIDTH), BF16),
        scratch_shapes=[
            pltpu.VMEM((heads, t + 16, hd), F32),
            pltpu.VMEM((heads, t + 16, hd), F32),
            pltpu.VMEM((heads, t + 16, hd), F32),
            pltpu.VMEM((heads, t, hd), F32),
            pltpu.VMEM((heads, t, hd), F32),
            pltpu.VMEM((heads, t, hd), F32),
            pltpu.VMEM((heads, nc, c, c), F32),
            pltpu.VMEM((heads, nc, c, c), F32),
            pltpu.VMEM((heads, 2, nc, 2 * c, hd), BF16),
            pltpu.VMEM((heads, 2, nc, 2 * c, hd), F32),
            pltpu.VMEM((heads, 2, nc * 8, LANES), F32),
            pltpu.VMEM((heads, t, hd), F32),
            pltpu.VMEM((c, 2 * c), F32),
            pltpu.VMEM((GDN_FULL_LEVELS - 1, c, 2 * c), BF16),
            pltpu.VMEM((c // 2, 2 * c), F32),
            pltpu.VMEM((3, c // 2, 2 * c), BF16),
            pltpu.VMEM((heads, 2, 2 * c, 2 * c), BF16),
            pltpu.VMEM((heads, 8, t), F32),
        ],
        compiler_params=pltpu.CompilerParams(
            dimension_semantics=("arbitrary", "arbitrary"), vmem_limit_bytes=VMEM_LIMIT),
        name="gdn",
    )(pbig, pbig, pbig, pbig, gch, gct, conv_w, conv_w, conv_w, norm_w)


def _gla_kernel(q_ref, k_ref, v_ref, g_ref, gc_ref, w2f_ref, w2b_ref, b2f_ref, b2b_ref, nw_ref,
                o_ref, tris, sst, osum, *, seq, unroll, heads):
    c2 = 2 * GLA_CHUNK
    npair = seq // c2
    dk = GLA_HEAD_K
    dv = GLA_HEAD_V
    ri = lax.broadcasted_iota(jnp.int32, (c2, c2), 0)
    ci = lax.broadcasted_iota(jnp.int32, (c2, c2), 1)
    same = (ri >> 6) == (ci >> 6)
    incls = [same & (ri >= ci), same & (ri <= ci)]
    crosses = [(ri >= GLA_CHUNK) & (ci < GLA_CHUNK), (ri < GLA_CHUNK) & (ci >= GLA_CHUNK)]
    row_k = lax.broadcasted_iota(jnp.int32, (c2, dk), 0)
    scanned_first = [row_k < GLA_CHUNK, row_k >= GLA_CHUNK]
    total_rows = [(GLA_CHUNK - 1, c2 - 1), (GLA_CHUNK, 0)]
    w2_refs = (w2f_ref, w2b_ref)
    b2_refs = (b2f_ref, b2b_ref)

    for d in range(2):
        tris[d] = jnp.where(incls[d], 1.0, 0.0).astype(BF16)
    osum[...] = jnp.zeros((seq, heads * dv), F32)
    sst[...] = jnp.zeros((heads * 2, dk, dv), F32)

    def pairs(it, _):
        items = []
        for u in range(unroll):
            for hh in range(heads):
                for d in range(2):
                    pp = it * unroll + u
                    pp = pp if d == 0 else npair - 1 - pp
                    rows = pl.ds(pl.multiple_of(pp * c2, c2), c2)
                    kc = slice(hh * dk, (hh + 1) * dk)
                    pre = _dot(gc_ref[rows, :], w2_refs[d][:, kc]) + b2_refs[d][:, kc]
                    items.append([(hh, d), rows, pre])
        for x in items:
            gk = _log_sigmoid(x[2]) / GLA_GATE_NORMALIZER
            cs = _dot(tris[x[0][1]], jnp.concatenate(_split2(gk), axis=1))
            x[2] = cs[:, :dk] + cs[:, dk:]
        for x in items:
            (hh, d), rows, gcum = x
            kc = slice(hh * dk, (hh + 1) * dk)
            r1, r2 = total_rows[d]
            tot1, tot2 = gcum[r1:r1 + 1, :], gcum[r2:r2 + 1, :]
            q = q_ref[rows, kc].astype(F32) * (dk ** -0.5)
            k = k_ref[rows, kc].astype(F32)
            qg = q * jnp.exp(gcum)
            qgb = qg.astype(BF16)
            kg = (k * jnp.exp(-gcum)).astype(BF16)
            kd = k * jnp.exp(jnp.where(scanned_first[d], tot1, tot2) - gcum)
            attn = jnp.where(incls[d], _dot_nt(qgb, kg),
                             jnp.where(crosses[d], _dot_nt(qgb, kd.astype(BF16)), 0.0))
            q_in = jnp.where(scanned_first[d], qg, qg * jnp.exp(tot1)).astype(BF16)
            kd_out = jnp.where(scanned_first[d], kd * jnp.exp(tot2), kd)
            gt = gcum.T
            e_col = jnp.exp(gt[:, r1:r1 + 1] + gt[:, r2:r2 + 1])
            x[2:] = [q_in, attn, kd_out.T.astype(BF16), e_col]
        for x in items:
            (hh, d), rows, q_in, attn, kdt, e_col = x
            vb = v_ref[rows, hh * dv:(hh + 1) * dv]
            x[3] = _dot(attn.astype(BF16), vb)
            x[4] = _dot(kdt, vb)
        states = [sst[i] for i in range(heads * 2)]
        for x in items:
            (hh, d), rows, q_in, intra, upd, e_col = x
            osum[rows, hh * dv:(hh + 1) * dv] += _dot(q_in, states[hh * 2 + d].astype(BF16)) + intra
            states[hh * 2 + d] = states[hh * 2 + d] * e_col + upd
        for i in range(heads * 2):
            sst[i] = states[i]
        return 0

    lax.fori_loop(0, npair // unroll, pairs, 0)

    for hh in range(heads):
        vc = slice(hh * dv, (hh + 1) * dv)
        o = osum[:, vc]
        y = o * lax.rsqrt(jnp.mean(o * o, axis=-1, keepdims=True) + NORM_EPS) * nw_ref[...]
        o_ref[:, vc] = (y * _silu(g_ref[:, vc].astype(F32))).astype(BF16)


def _gla(pbig, gc, w2f, w2b, b2f, b2b, norm_w, *, unroll=4, heads=4):
    b, t, _ = pbig.shape
    dk = GLA_HEAD_K
    dv = GLA_HEAD_V
    kw, vw = heads * dk, heads * dv
    assert GLA_HEADS % heads == 0
    return pl.pallas_call(
        functools.partial(_gla_kernel, seq=t, unroll=unroll, heads=heads),
        grid=(b, GLA_HEADS // heads),
        in_specs=[
            pl.BlockSpec((None, t, kw), lambda i, j: (i, 0, COL_QB // kw + j)),
            pl.BlockSpec((None, t, kw), lambda i, j: (i, 0, COL_KB // kw + j)),
            pl.BlockSpec((None, t, vw), lambda i, j: (i, 0, COL_VB // vw + j)),
            pl.BlockSpec((None, t, vw), lambda i, j: (i, 0, COL_GB // vw + j)),
            pl.BlockSpec((None, t, LANES), lambda i, j: (i, 0, 0)),
            pl.BlockSpec((LANES, kw), lambda i, j: (0, j)),
            pl.BlockSpec((LANES, kw), lambda i, j: (0, j)),
            pl.BlockSpec((1, kw), lambda i, j: (0, j)),
            pl.BlockSpec((1, kw), lambda i, j: (0, j)),
            pl.BlockSpec((1, dv), lambda i, j: (0, 0)),
        ],
        out_specs=pl.BlockSpec((None, t, vw), lambda i, j: (i, 0, j)),
        out_shape=jax.ShapeDtypeStruct((b, t, GLA_VAL_DIM), BF16),
        scratch_shapes=[
            pltpu.VMEM((2, 2 * GLA_CHUNK, 2 * GLA_CHUNK), BF16),
            pltpu.VMEM((heads * 2, dk, dv), F32),
            pltpu.VMEM((t, vw), F32),
        ],
        compiler_params=pltpu.CompilerParams(
            dimension_semantics=("arbitrary", "arbitrary"), vmem_limit_bytes=VMEM_LIMIT),
        name="gla",
    )(pbig, pbig, pbig, pbig, gc, w2f, w2b, b2f, b2b, norm_w)


def _merge_kernel(oa_ref, ob_ref, ga_ref, gb_ref, x_ref, wa_ref, wb_ref, wo_ref, lnw_ref, out_ref):
    ya = _dot(oa_ref[...], wa_ref[...])
    yb = _dot(ob_ref[...], wb_ref[...])
    merged = (_sigmoid(ga_ref[...].astype(F32)) * ya + _sigmoid(gb_ref[...].astype(F32)) * yb)
    out = _dot(merged.astype(BF16), wo_ref[...])
    y = out * lax.rsqrt(jnp.mean(out * out, axis=-1, keepdims=True) + NORM_EPS) * lnw_ref[...]
    out_ref[...] = x_ref[...] + y


def _merge(oa, ob, pbig2d, x2d, wa, wb, wo, ln_w, *, tm=512):
    m, d = x2d.shape
    return pl.pallas_call(
        _merge_kernel,
        grid=(m // tm,),
        in_specs=[
            pl.BlockSpec((tm, d), lambda i: (i, 0)),
            pl.BlockSpec((tm, d), lambda i: (i, 0)),
            pl.BlockSpec((tm, d), lambda i: (i, COL_GATE_A // D_MODEL)),
            pl.BlockSpec((tm, d), lambda i: (i, COL_GATE_B // D_MODEL)),
            pl.BlockSpec((tm, d), lambda i: (i, 0)),
            pl.BlockSpec((d, d), lambda i: (0, 0)),
            pl.BlockSpec((d, d), lambda i: (0, 0)),
            pl.BlockSpec((d, d), lambda i: (0, 0)),
            pl.BlockSpec((1, d), lambda i: (0, 0)),
        ],
        out_specs=pl.BlockSpec((tm, d), lambda i: (i, 0)),
        out_shape=jax.ShapeDtypeStruct((m, d), F32),
        compiler_params=pltpu.CompilerParams(
            dimension_semantics=("arbitrary",), vmem_limit_bytes=VMEM_LIMIT),
        name="merge",
    )(oa, ob, pbig2d, pbig2d, x2d, wa, wb, wo, ln_w)


def _layer(x, ln_pre_w, w_in, conv_w, a_log_fwd, a_log_bwd, dt_bias_fwd, dt_bias_bwd,
           gdn_norm_w, w_proj_gdn, gk_w2_fwd, gk_b2_fwd, gk_w2_bwd, gk_b2_bwd,
           gla_norm_w, w_proj_gla, w_out, ln_post_w):
    b, t, d = x.shape
    m = b * t
    w_big, w_small = _wprep(w_in.T)
    pad = jnp.zeros((LANES - 2 * GDN_HEADS,), F32)
    alog_vec = jnp.concatenate([a_log_fwd, a_log_bwd, pad]).reshape(1, LANES)
    dtb_vec = jnp.concatenate([dt_bias_fwd, dt_bias_bwd, pad]).reshape(1, LANES)

    def w2_ext(w2, lane0):
        z = jnp.zeros((LANES, GLA_KEY_DIM), F32)
        return z.at[lane0:lane0 + GLA_GATE_RANK, :].set(w2).astype(BF16)

    x2d = x.reshape(m, d)
    pbig, psmall = _inproj(x2d, ln_pre_w.reshape(1, d), w_big, w_small)
    gch, gct = _gates(psmall.reshape(b, t, LANES), alog_vec, dtb_vec)
    pbig3 = pbig.reshape(b, t, N_BIG)
    oa = _gdn(pbig3, gch, gct, conv_w, gdn_norm_w.reshape(1, GDN_HEAD_DIM))
    ob = _gla(pbig3, gch, w2_ext(gk_w2_fwd, LANE_RF), w2_ext(gk_w2_bwd, LANE_RB),
              gk_b2_fwd.reshape(1, GLA_KEY_DIM), gk_b2_bwd.reshape(1, GLA_KEY_DIM),
              gla_norm_w.reshape(1, GLA_HEAD_V))
    out = _merge(oa.reshape(m, GDN_WIDTH), ob.reshape(m, GLA_VAL_DIM), pbig, x2d,
                 w_proj_gdn.astype(BF16), w_proj_gla.astype(BF16), w_out.astype(BF16),
                 ln_post_w.reshape(1, d))
    return out.reshape(b, t, d)


def kernel(x, ln_pre_w, w_in, conv_w, a_log_fwd, a_log_bwd, dt_bias_fwd, dt_bias_bwd, gdn_norm_w,
           w_proj_gdn, gk_w2_fwd, gk_b2_fwd, gk_w2_bwd, gk_b2_bwd, gla_norm_w, w_proj_gla, w_out,
           ln_post_w):
    h = x
    for l in range(ln_pre_w.shape[0]):
        h = _layer(h, ln_pre_w[l], w_in[l], conv_w[l], a_log_fwd[l], a_log_bwd[l],
                   dt_bias_fwd[l], dt_bias_bwd[l], gdn_norm_w[l], w_proj_gdn[l],
                   gk_w2_fwd[l], gk_b2_fwd[l], gk_w2_bwd[l], gk_b2_bwd[l],
                   gla_norm_w[l], w_proj_gla[l], w_out[l], ln_post_w[l])
    return h
```
